```python
import math
import jax, jax.numpy as jnp
from jax import lax
import numpy as np

D_MODEL = 1024
BATCH = 4
SEQ = 8192
DEPTH = 2

CHUNK = 64
D_MIX = D_MODEL
EPS = 1e-6

A_HEADS = 4
A_HEAD_DIM = 64
A_LEFT_CHUNKS = 8
A_BAND = (A_LEFT_CHUNKS + 1) * CHUNK
REL_PAST = 256
REL_FUTURE = CHUNK - 1
N_REL = REL_PAST + REL_FUTURE + 1

B_HEADS = 4
B_HEAD_DIM = 64
Q_BLOCK = 128
FGATE_BIAS_OFFSET = 2.0

C_HEADS = 4
C_HEAD_DIM = 128
C_CONV = 4

D_FF = 2816
FFN_CONV = 3

A_W = A_HEADS * A_HEAD_DIM
B_W = B_HEADS * B_HEAD_DIM
C_W = C_HEADS * C_HEAD_DIM

A_QKV_END = 3 * A_W
B_QKV_END = A_QKV_END + 3 * B_W
B_F_END = B_QKV_END + B_HEADS
C_QKV_END = B_F_END + 3 * C_W
C_BETA_END = C_QKV_END + C_HEADS
C_A_END = C_BETA_END + C_HEADS
D_IN = C_A_END + C_W
SPLIT_POINTS = (A_QKV_END, B_QKV_END, B_F_END, C_QKV_END, C_BETA_END, C_A_END)

kernel_name = "hybrid_chunk_stream_encoder"


def rms_norm(x, g):
    xf = x.astype(jnp.float32)
    y = xf * lax.rsqrt(jnp.mean(xf * xf, axis=-1, keepdims=True) + EPS)
    return (y * g.astype(jnp.float32)).astype(x.dtype)


def l2_normalize(x):
    return x * lax.rsqrt(jnp.sum(x * x, axis=-1, keepdims=True) + EPS)


def causal_dwconv(x, w):
    K = w.shape[0]
    T = x.shape[1]
    xp = jnp.pad(x, ((0, 0), (K - 1, 0), (0, 0)))
    return sum(xp[:, i:i + T] * w[i] for i in range(K))


def chunk_relpos_attention(q, k, v, rel_table):
    Bsz, T, H, Dh = q.shape
    nC = T // CHUNK
    qc = q.reshape(Bsz, nC, CHUNK, H, Dh)
    pad = ((0, 0), (A_LEFT_CHUNKS, 0), (0, 0), (0, 0), (0, 0))
    kc = jnp.pad(k.reshape(Bsz, nC, CHUNK, H, Dh), pad)
    vc = jnp.pad(v.reshape(Bsz, nC, CHUNK, H, Dh), pad)
    kb = jnp.concatenate([kc[:, j:j + nC] for j in range(A_LEFT_CHUNKS + 1)], axis=2)
    vb = jnp.concatenate([vc[:, j:j + nC] for j in range(A_LEFT_CHUNKS + 1)], axis=2)
    s = jnp.einsum('bnqhd,bnkhd->bhnqk', qc, kb,
                   preferred_element_type=jnp.float32) * (Dh ** -0.5)
    qi = jnp.arange(CHUNK)[:, None]
    kk = jnp.arange(A_BAND)[None, :]
    rel = qi - (kk - A_LEFT_CHUNKS * CHUNK)
    idx = jnp.clip(rel, -REL_FUTURE, REL_PAST) + REL_FUTURE
    bias = rel_table.astype(jnp.float32)[:, idx]
    s = s + bias[None, :, None, :, :]
    chunk_id = jnp.arange(nC)[:, None]
    slot = (jnp.arange(A_BAND) // CHUNK)[None, :]
    valid = (chunk_id + slot - A_LEFT_CHUNKS) >= 0
    s = jnp.where(valid[None, None, :, None, :], s, -jnp.inf)
    p = jax.nn.softmax(s, axis=-1)
    o = jnp.einsum('bhnqk,bnkhd->bnqhd', p.astype(v.dtype), vb)
    return o.reshape(Bsz, T, H * Dh)


def forgetting_attention(q, k, v, f_logit):
    Bsz, T, H, Dh = q.shape
    logf = jax.nn.log_sigmoid(f_logit.astype(jnp.float32))
    F = jnp.cumsum(logf, axis=1)
    Fk = F.transpose(0, 2, 1)
    nQ = T // Q_BLOCK
    qb = q.reshape(Bsz, nQ, Q_BLOCK, H, Dh).transpose(1, 0, 2, 3, 4)
    Fq = F.reshape(Bsz, nQ, Q_BLOCK, H).transpose(1, 0, 3, 2)
    kpos = jnp.arange(T)
    scale = Dh ** -0.5

    def block(args):
        q_i, Fq_i, i = args
        s = jnp.einsum('bqhd,bkhd->bhqk', q_i, k,
                       preferred_element_type=jnp.float32) * scale
        s = s + Fq_i[:, :, :, None] - Fk[:, :, None, :]
        qpos = i * Q_BLOCK + jnp.arange(Q_BLOCK)
        s = jnp.where(kpos[None, :] <= qpos[:, None], s, -jnp.inf)
        p = jax.nn.softmax(s, axis=-1)
        return jnp.einsum('bhqk,bkhd->bqhd', p.astype(v.dtype), v)

    o = lax.map(block, (qb, Fq, jnp.arange(nQ)))
    return o.transpose(1, 0, 2, 3, 4).reshape(Bsz, T, H * Dh)


def gated_deltanet(q, k, v, beta_logit, a_logit, A_log, dt_bias):
    f32 = jnp.float32
    Bsz, T, H, Dk = q.shape
    Dv = v.shape[-1]
    nC = T // CHUNK
    q = l2_normalize(q.astype(f32)) * (Dk ** -0.5)
    k = l2_normalize(k.astype(f32))
    v = v.astype(f32)
    beta = jax.nn.sigmoid(beta_logit.astype(f32))
    g = -jnp.exp(A_log.astype(f32)) * jax.nn.softplus(a_logit.astype(f32) + dt_bias.astype(f32))

    def to_chunks(t):
        return t.reshape(Bsz, nC, CHUNK, H, -1).transpose(0, 3, 1, 2, 4)

    def to_chunks_h(t):
        return t.reshape(Bsz, nC, CHUNK, H).transpose(0, 3, 1, 2)

    qc, kc, vc = to_chunks(q), to_chunks(k), to_chunks(v)
    bc = to_chunks_h(beta)
    gc = jnp.cumsum(to_chunks_h(g), axis=-1)
    kbeta = kc * bc[..., None]
    vbeta = vc * bc[..., None]
    tril = jnp.tril(jnp.ones((CHUNK, CHUNK), dtype=bool))
    strict = jnp.tril(jnp.ones((CHUNK, CHUNK), dtype=bool), k=-1)
    diff = gc[..., :, None] - gc[..., None, :]
    Lm = jnp.where(tril, jnp.exp(jnp.where(tril, diff, 0.0)), 0.0)
    Amat = jnp.where(strict, jnp.einsum('bhncd,bhnsd->bhncs', kbeta, kc) * Lm, 0.0)
    eye = jnp.eye(CHUNK, dtype=f32)
    rhs = jnp.concatenate([vbeta, kbeta * jnp.exp(gc)[..., None]], axis=-1)
    sol = lax.linalg.triangular_solve(eye + Amat, rhs, left_side=True, lower=True,
                                      unit_diagonal=True)
    U, W = sol[..., :Dv], sol[..., Dv:]
    attn = jnp.where(tril, jnp.einsum('bhncd,bhnsd->bhncs', qc, kc) * Lm, 0.0)

    def step(S, inp):
        q_i, k_i, u_i, w_i, g_i, a_i = inp
        v_new = u_i - jnp.einsum('bhck,bhkv->bhcv', w_i, S)
        o = (jnp.einsum('bhck,bhkv->bhcv', q_i * jnp.exp(g_i)[..., None], S)
             + jnp.einsum('bhcs,bhsv->bhcv', a_i, v_new))
        g_last = g_i[..., -1]
        k_dec = k_i * jnp.exp(g_last[..., None] - g_i)[..., None]
        S = S * jnp.exp(g_last)[..., None, None] + jnp.einsum('bhck,bhcv->bhkv', k_dec, v_new)
        return S, o

    xs = tuple(jnp.moveaxis(t, 2, 0) for t in (qc, kc, U, W, gc, attn))
    S0 = jnp.zeros((Bsz, H, Dk, Dv), f32)
    _, o = lax.scan(step, S0, xs)
    return o.transpose(1, 0, 3, 2, 4).reshape(Bsz, T, H, Dv)


def hybrid_mixer(h, w_in, w_out, rel_bias, fgate_bias, conv_w, A_log, dt_bias, gnorm_g):
    Bsz, T, _ = h.shape
    proj = h @ w_in
    qkv_a, qkv_b, f_logit, qkv_c, beta_logit, a_logit, z = jnp.split(proj, SPLIT_POINTS, axis=-1)
    qa, ka, va = [t.reshape(Bsz, T, A_HEADS, A_HEAD_DIM) for t in jnp.split(qkv_a, 3, axis=-1)]
    o_a = chunk_relpos_attention(qa, ka, va, rel_bias)
    qb, kb, vb = [t.reshape(Bsz, T, B_HEADS, B_HEAD_DIM) for t in jnp.split(qkv_b, 3, axis=-1)]
    o_b = forgetting_attention(qb, kb, vb, f_logit + fgate_bias)
    qkv_c = jax.nn.silu(causal_dwconv(qkv_c, conv_w))
    qc, kc, vc = [t.reshape(Bsz, T, C_HEADS, C_HEAD_DIM) for t in jnp.split(qkv_c, 3, axis=-1)]
    o_c = gated_deltanet(qc, kc, vc, beta_logit, a_logit, A_log, dt_bias)
    zc = z.reshape(Bsz, T, C_HEADS, C_HEAD_DIM).astype(jnp.float32)
    o_c = (rms_norm(o_c, gnorm_g) * jax.nn.silu(zc)).astype(h.dtype).reshape(Bsz, T, C_W)
    return jnp.concatenate([o_a, o_b, o_c], axis=-1) @ w_out


def conv_gated_mlp(h, w_up, conv_w, w_down):
    u = causal_dwconv(h @ w_up, conv_w)
    gate, up = jnp.split(u, 2, axis=-1)
    return (jax.nn.silu(gate) * up) @ w_down


def setup_inputs(seed: int = 0) -> dict:
    key = jax.random.key(seed)
    ks = jax.random.split(key, 20)
    f32 = jnp.float32
    nrm = lambda k, shape, s: jax.random.normal(k, shape, f32) * s
    x = nrm(ks[0], (BATCH, SEQ, D_MODEL), 1.0)
    c = nrm(ks[1], (BATCH, D_MODEL), 1.0)
    ada_w = nrm(ks[2], (DEPTH, D_MODEL, 6 * D_MODEL), 0.5 * D_MODEL ** -0.5)
    ada_b = nrm(ks[3], (DEPTH, 6 * D_MODEL), 0.02)
    norm_mix_g = 1.0 + nrm(ks[4], (DEPTH, D_MODEL), 0.02)
    norm_ffn_g = 1.0 + nrm(ks[5], (DEPTH, D_MODEL), 0.02)
    w_in = nrm(ks[6], (DEPTH, D_MODEL, D_IN), D_MODEL ** -0.5)
    w_out = nrm(ks[7], (DEPTH, D_MIX, D_MODEL), D_MIX ** -0.5)
    rel_bias = nrm(ks[8], (DEPTH, A_HEADS, N_REL), 0.5)
    fgate_bias = FGATE_BIAS_OFFSET + nrm(ks[9], (DEPTH, B_HEADS), 0.5)
    gdn_conv_w = nrm(ks[10], (DEPTH, C_CONV, 3 * C_W), C_CONV ** -0.5)
    gdn_A_log = jnp.log(jax.random.uniform(ks[11], (DEPTH, C_HEADS), f32, 1.0, 16.0))
    dt = jnp.exp(jax.random.uniform(ks[12], (DEPTH, C_HEADS), f32,
                                    math.log(1e-3), math.log(1e-1)))
    gdn_dt_bias = dt + jnp.log(-jnp.expm1(-dt))
    gdn_norm_g = 1.0 + nrm(ks[13], (DEPTH, C_HEAD_DIM), 0.02)
    ffn_w_up = nrm(ks[14], (DEPTH, D_MODEL, 2 * D_FF), D_MODEL ** -0.5)
    ffn_conv_w = nrm(ks[15], (DEPTH, FFN_CONV, 2 * D_FF), FFN_CONV ** -0.5)
    ffn_w_down = nrm(ks[16], (DEPTH, D_FF, D_MODEL), D_FF ** -0.5)
    final_norm_g = 1.0 + nrm(ks[17], (D_MODEL,), 0.02)
    return {"x": x, "c": c, "ada_w": ada_w, "ada_b": ada_b,
            "norm_mix_g": norm_mix_g, "norm_ffn_g": norm_ffn_g,
            "w_in": w_in, "w_out": w_out, "rel_bias": rel_bias, "fgate_bias": fgate_bias,
            "gdn_conv_w": gdn_conv_w, "gdn_A_log": gdn_A_log, "gdn_dt_bias": gdn_dt_bias,
            "gdn_norm_g": gdn_norm_g, "ffn_w_up": ffn_w_up, "ffn_conv_w": ffn_conv_w,
            "ffn_w_down": ffn_w_down, "final_norm_g": final_norm_g}


def reference(x, c, ada_w, ada_b, norm_mix_g, norm_ffn_g, w_in, w_out, rel_bias, fgate_bias,
              gdn_conv_w, gdn_A_log, gdn_dt_bias, gdn_norm_g, ffn_w_up, ffn_conv_w,
              ffn_w_down, final_norm_g):
    c_act = jax.nn.silu(c)
    for l in range(DEPTH):
        mod = (c_act @ ada_w[l] + ada_b[l])[:, None, :]
        shift1, scale1, gate1, shift2, scale2, gate2 = jnp.split(mod, 6, axis=-1)
        h = rms_norm(x, norm_mix_g[l]) * (1.0 + scale1) + shift1
        x = x + gate1 * hybrid_mixer(h, w_in[l], w_out[l], rel_bias[l], fgate_bias[l],
                                     gdn_conv_w[l], gdn_A_log[l], gdn_dt_bias[l], gdn_norm_g[l])
        h = rms_norm(x, norm_ffn_g[l]) * (1.0 + scale2) + shift2
        x = x + gate2 * conv_gated_mlp(h, ffn_w_up[l], ffn_conv_w[l], ffn_w_down[l])
    return rms_norm(x, final_norm_g)
```

```python
import functools

import jax
import jax.numpy as jnp
from jax import lax
from jax.experimental import pallas as pl
from jax.experimental.pallas import tpu as pltpu

F32 = jnp.float32
BF16 = jnp.bfloat16

D_MODEL = 1024
DEPTH = 2
CHUNK = 64
EPS = 1e-6
A_HEADS = 4
A_HEAD_DIM = 64
A_LEFT_CHUNKS = 8
REL_PAST = 256
REL_FUTURE = CHUNK - 1
B_HEADS = 4
B_HEAD_DIM = 64
C_HEADS = 4
C_HEAD_DIM = 128
C_CONV = 4
D_FF = 2816
FFN_CONV = 3
A_W = A_HEADS * A_HEAD_DIM
B_W = B_HEADS * B_HEAD_DIM
C_W = C_HEADS * C_HEAD_DIM
AB_W = 3 * A_W + 3 * B_W
C3_W = 3 * C_W
LANES = 128
GATE_W = LANES
F_LANE, BETA_LANE, A_LANE = 0, 4, 8
NEG = -1e30
VMEM_LIMIT = 56 * 1024 * 1024

TM_PROJ = 512
TF_GATE = 1024
TQ_A = 512
TQ_B = 512
TM_GDN = 256
TM_FFN = 512
FF_CH = 256
N_FF_CH = D_FF // FF_CH


def _sigmoid(x):
    return 1.0 / (1.0 + jnp.exp(-x))


def _silu(x):
    return x * _sigmoid(x)


def _softplus(x):
    return jnp.maximum(x, 0.0) + jnp.log1p(jnp.exp(-jnp.abs(x)))


def _log_sigmoid(x):
    return jnp.minimum(x, 0.0) - jnp.log1p(jnp.exp(-jnp.abs(x)))


def _rms(x, g):
    return x * lax.rsqrt(jnp.mean(x * x, axis=-1, keepdims=True) + EPS) * g


def _params(sem):
    return pltpu.CompilerParams(dimension_semantics=sem, vmem_limit_bytes=VMEM_LIMIT)


def _const_spec(shape):
    nd = len(shape)
    return pl.BlockSpec(shape, lambda *_: (0,) * nd, pipeline_mode=pl.Buffered(1))


def _mod_kernel(c_ref, w_ref, b_ref, o_ref):
    c_act = _silu(c_ref[...])
    o_ref[0] = jnp.dot(c_act, w_ref[0], preferred_element_type=F32) + b_ref[0]


def _modulation(c, ada_w, ada_b):
    bsz = c.shape[0]
    rows = 8
    cp = jnp.zeros((rows, D_MODEL), F32).at[:bsz].set(c)
    ncol = 6 * D_MODEL
    cb = 1536
    out = pl.pallas_call(
        _mod_kernel,
        grid=(DEPTH, ncol // cb),
        in_specs=[
            pl.BlockSpec((rows, D_MODEL), lambda l, j: (0, 0)),
            pl.BlockSpec((1, D_MODEL, cb), lambda l, j: (l, 0, j)),
            pl.BlockSpec((1, 1, cb), lambda l, j: (l, 0, j)),
        ],
        out_specs=pl.BlockSpec((1, rows, cb), lambda l, j: (l, 0, j)),
        out_shape=jax.ShapeDtypeStruct((DEPTH, rows, ncol), F32),
        compiler_params=_params(("arbitrary", "arbitrary")),
        name="adaln_modulation",
    )(cp, ada_w, ada_b.reshape(DEPTH, 1, ncol))
    return out[:, :bsz].reshape(DEPTH, bsz, 6, D_MODEL)


_PROJ_W = AB_W + C3_W + C_W + GATE_W


def _inproj_kernel(x_ref, mod_ref, g_ref, w_ref, oab_ref, oc_ref, oz_ref, og_ref):
    x = x_ref[0]
    shift = mod_ref[0, 0:1, :]
    scale = mod_ref[0, 1:2, :]
    h = _rms(x, g_ref[...]) * (1.0 + scale) + shift
    hb = h.astype(BF16)
    cw = 512
    for j in range(AB_W // cw):
        oab_ref[0, :, j * cw:(j + 1) * cw] = jnp.dot(
            hb, w_ref[:, j * cw:(j + 1) * cw], preferred_element_type=F32).astype(BF16)
    for j in range(C3_W // cw):
        oc_ref[0, :, j * cw:(j + 1) * cw] = jnp.dot(
            hb, w_ref[:, AB_W + j * cw:AB_W + (j + 1) * cw], preferred_element_type=F32)
    oz_ref[0] = jnp.dot(hb, w_ref[:, AB_W + C3_W:AB_W + C3_W + C_W],
                        preferred_element_type=F32)
    og_ref[0] = jnp.dot(hb, w_ref[:, AB_W + C3_W + C_W:], preferred_element_type=F32)


def _inproj(x, mod_l, g, w):
    bsz, t, _ = x.shape
    tm = TM_PROJ
    row = lambda b, i: (b, i, 0)
    return pl.pallas_call(
        _inproj_kernel,
        grid=(bsz, t // tm),
        in_specs=[
            pl.BlockSpec((1, tm, D_MODEL), row),
            pl.BlockSpec((1, 6, D_MODEL), lambda b, i: (b, 0, 0)),
            _const_spec((1, D_MODEL)),
            _const_spec((D_MODEL, _PROJ_W)),
        ],
        out_specs=[
            pl.BlockSpec((1, tm, AB_W), row),
            pl.BlockSpec((1, tm, C3_W), row),
            pl.BlockSpec((1, tm, C_W), row),
            pl.BlockSpec((1, tm, GATE_W), row),
        ],
        out_shape=[
            jax.ShapeDtypeStruct((bsz, t, AB_W), BF16),
            jax.ShapeDtypeStruct((bsz, t, C3_W), F32),
            jax.ShapeDtypeStruct((bsz, t, C_W), F32),
            jax.ShapeDtypeStruct((bsz, t, GATE_W), F32),
        ],
        compiler_params=_params(("parallel", "parallel")),
        name="inproj",
    )(x, mod_l, g, w)


def _arrange_w_in(w_in_l):
    a_end = 3 * A_W
    b_end = a_end + 3 * B_W
    f_end = b_end + B_HEADS
    c_end = f_end + C3_W
    beta_end = c_end + C_HEADS
    a_gate_end = beta_end + C_HEADS
    qa_scale = A_HEAD_DIM ** -0.5
    qb_scale = B_HEAD_DIM ** -0.5
    w_ab = jnp.concatenate([
        w_in_l[:, 0:A_W] * qa_scale, w_in_l[:, A_W:a_end],
        w_in_l[:, a_end:a_end + B_W] * qb_scale, w_in_l[:, a_end + B_W:b_end]], axis=1)
    w_gate = jnp.concatenate([
        w_in_l[:, b_end:f_end], w_in_l[:, c_end:beta_end], w_in_l[:, beta_end:a_gate_end],
        jnp.zeros((D_MODEL, GATE_W - B_HEADS - 2 * C_HEADS), F32)], axis=1)
    w = jnp.concatenate([w_ab, w_in_l[:, f_end:c_end], w_in_l[:, a_gate_end:], w_gate], axis=1)
    return w.astype(BF16)


def _fgate_kernel(g_ref, fb_ref, o_ref, carry_ref):
    @pl.when(pl.program_id(1) == 0)
    def _():
        carry_ref[...] = jnp.zeros_like(carry_ref)

    lf = _log_sigmoid(g_ref[0] + fb_ref[...])
    rows = lf.T[0:8, :]
    tf = rows.shape[1]
    lane = lax.broadcasted_iota(jnp.int32, rows.shape, 1)
    d = 1
    while d < tf:
        rows = rows + jnp.where(lane >= d, pltpu.roll(rows, d, axis=1), 0.0)
        d *= 2
    out = rows + carry_ref[:, 0:1]
    o_ref[0] = out
    carry_ref[...] = jnp.broadcast_to(out[:, tf - 1:tf], carry_ref.shape)


def _fgate(gates, fbias_lane):
    bsz, t, _ = gates.shape
    tf = min(TF_GATE, t)
    return pl.pallas_call(
        _fgate_kernel,
        grid=(bsz, t // tf),
        in_specs=[
            pl.BlockSpec((1, tf, GATE_W), lambda b, i: (b, i, 0)),
            _const_spec((1, GATE_W)),
        ],
        out_specs=pl.BlockSpec((1, 8, tf), lambda b, i: (b, 0, i)),
        out_shape=jax.ShapeDtypeStruct((bsz, 8, t), F32),
        scratch_shapes=[pltpu.VMEM((8, LANES), F32)],
        compiler_params=_params(("parallel", "arbitrary")),
        name="fgate_cumsum",
    )(gates, fbias_lane)


def _mixa_kernel(q_ref, kp_ref, kc_ref, vp_ref, vc_ref, bias_ref, o_ref):
    i = pl.program_id(1)
    tq = q_ref.shape[1]
    lane = lax.broadcasted_iota(jnp.int32, (1, LANES), 1)
    lo = lane < A_HEAD_DIM
    has_prev = i > 0
    nt = (((1,), (1,)), ((), ()))
    for pair in range(A_HEADS // 2):
        sl = slice(pair * LANES, (pair + 1) * LANES)
        q = q_ref[0, :, sl]
        kp, kc = kp_ref[0, :, sl], kc_ref[0, :, sl]
        vp, vc = vp_ref[0, :, sl], vc_ref[0, :, sl]
        outs = []
        for hh in range(2):
            h = 2 * pair + hh
            qh = jnp.where(lo if hh == 0 else ~lo, q, jnp.zeros_like(q))
            s_p = lax.dot_general(qh, kp, nt, preferred_element_type=F32) + bias_ref[h, :, 0:tq]
            s_p = jnp.where(has_prev, s_p, NEG)
            s_c = lax.dot_general(qh, kc, nt, preferred_element_type=F32) + bias_ref[h, :, tq:2 * tq]
            m = jnp.maximum(jnp.max(s_p, axis=-1, keepdims=True),
                            jnp.max(s_c, axis=-1, keepdims=True))
            p_p = jnp.exp(s_p - m)
            p_c = jnp.exp(s_c - m)
            l = jnp.sum(p_p, axis=-1, keepdims=True) + jnp.sum(p_c, axis=-1, keepdims=True)
            o = (jnp.dot(p_p.astype(BF16), vp, preferred_element_type=F32)
                 + jnp.dot(p_c.astype(BF16), vc, preferred_element_type=F32))
            outs.append(o / l)
        o_ref[0, :, sl] = jnp.where(lo, outs[0], outs[1]).astype(BF16)


def _mixer_a(qkv_ab, bias):
    bsz, t, _ = qkv_ab.shape
    tq = TQ_A
    prev = lambda col: (lambda b, i: (b, jnp.maximum(i - 1, 0), col))
    cur = lambda col: (lambda b, i: (b, i, col))
    blk = (1, tq, A_W)
    return pl.pallas_call(
        _mixa_kernel,
        grid=(bsz, t // tq),
        in_specs=[
            pl.BlockSpec(blk, cur(0)),
            pl.BlockSpec(blk, prev(1)), pl.BlockSpec(blk, cur(1)),
            pl.BlockSpec(blk, prev(2)), pl.BlockSpec(blk, cur(2)),
            _const_spec((A_HEADS, tq, 2 * tq)),
        ],
        out_specs=pl.BlockSpec(blk, cur(0)),
        out_shape=jax.ShapeDtypeStruct((bsz, t, A_W), BF16),
        compiler_params=_params(("parallel", "parallel")),
        name="mixer_a_chunk_attn",
    )(qkv_ab, qkv_ab, qkv_ab, qkv_ab, qkv_ab, bias)


def _rel_bias_tile(rel_table):
    tq = TQ_A
    qi = jnp.arange(tq)[:, None]
    kpos = jnp.arange(2 * tq)[None, :] - tq
    rel = qi - kpos
    idx = jnp.clip(rel, -REL_FUTURE, REL_PAST) + REL_FUTURE
    qc = qi // CHUNK
    kc = jnp.floor_divide(kpos, CHUNK)
    valid = (kc <= qc) & (kc >= qc - A_LEFT_CHUNKS)
    return jnp.where(valid[None], rel_table.astype(F32)[:, idx], NEG)


def _mixb_kernel(q_ref, k_ref, v_ref, f_ref, o_ref, m_ref, l_ref, acc_ref):
    pair = pl.program_id(1)
    i = pl.program_id(2)
    tq = q_ref.shape[1]
    tk = tq
    lane = lax.broadcasted_iota(jnp.int32, (1, LANES), 1)
    lo = lane < B_HEAD_DIM
    q = q_ref[0]
    qs = (jnp.where(lo, q, jnp.zeros_like(q)), jnp.where(lo, jnp.zeros_like(q), q))
    nt = (((1,), (1,)), ((), ()))
    m_ref[...] = jnp.full(m_ref.shape, NEG, F32)
    l_ref[...] = jnp.zeros(l_ref.shape, F32)
    acc_ref[...] = jnp.zeros(acc_ref.shape, F32)

    def block(j, diagonal):
        off = pl.multiple_of(j * tk, tk)
        k = k_ref[0, pl.ds(off, tk), :]
        v = v_ref[0, pl.ds(off, tk), :]
        for hh in range(2):
            fk = f_ref[0, pl.ds(2 * pair + hh, 1), pl.ds(off, tk)]
            s = lax.dot_general(qs[hh], k, nt, preferred_element_type=F32) - fk
            if diagonal:
                r = lax.broadcasted_iota(jnp.int32, (tq, tk), 0)
                c = lax.broadcasted_iota(jnp.int32, (tq, tk), 1)
                s = jnp.where(c <= r, s, NEG)
            m_prev = m_ref[hh]
            m_new = jnp.maximum(m_prev, jnp.max(s, axis=-1, keepdims=True))
            alpha = jnp.exp(m_prev - m_new)
            p = jnp.exp(s - m_new)
            l_ref[hh] = alpha * l_ref[hh] + jnp.sum(p, axis=-1, keepdims=True)
            acc_ref[hh] = alpha * acc_ref[hh] + jnp.dot(p.astype(BF16), v,
                                                        preferred_element_type=F32)
            m_ref[hh] = m_new

    def body(j, carry):
        block(j, False)
        return carry

    lax.fori_loop(0, i, body, 0)
    block(i, True)
    o_ref[0] = jnp.where(lo, acc_ref[0] / l_ref[0], acc_ref[1] / l_ref[1]).astype(BF16)


def _mixer_b(qkv_ab, f_rows):
    bsz, t, _ = qkv_ab.shape
    tq = TQ_B
    base = 3 * A_W // LANES
    npair = B_HEADS // 2
    return pl.pallas_call(
        _mixb_kernel,
        grid=(bsz, npair, t // tq),
        in_specs=[
            pl.BlockSpec((1, tq, LANES), lambda b, p, i: (b, i, base + p)),
            pl.BlockSpec((1, t, LANES), lambda b, p, i: (b, 0, base + npair + p)),
            pl.BlockSpec((1, t, LANES), lambda b, p, i: (b, 0, base + 2 * npair + p)),
            pl.BlockSpec((1, 8, t), lambda b, p, i: (b, 0, 0)),
        ],
        out_specs=pl.BlockSpec((1, tq, LANES), lambda b, p, i: (b, i, p)),
        out_shape=jax.ShapeDtypeStruct((bsz, t, B_W), BF16),
        scratch_shapes=[
            pltpu.VMEM((2, tq, 1), F32),
            pltpu.VMEM((2, tq, 1), F32),
            pltpu.VMEM((2, tq, LANES), F32),
        ],
        compiler_params=_params(("parallel", "parallel", "parallel")),
        name="mixer_b_forgetting_attn",
    )(qkv_ab, qkv_ab, qkv_ab, f_rows)


_HI = lax.Precision.HIGHEST


def _gdn_kernel(x_ref, gate_ref, z_ref, cw_ref, alog_ref, dtb_ref, gn_ref, o_ref,
                xs_ref, s_ref, vn_ref):
    tm = x_ref.shape[1]
    nchunk = tm // CHUNK

    @pl.when(pl.program_id(1) == 0)
    def _():
        xs_ref[0:8, :] = jnp.zeros((8, C3_W), F32)
        s_ref[...] = jnp.zeros_like(s_ref)

    xs_ref[8:8 + tm, :] = x_ref[0]
    y = cw_ref[0:1, :] * xs_ref[5:5 + tm, :]
    for tap in range(1, C_CONV):
        y = y + cw_ref[tap:tap + 1, :] * xs_ref[5 + tap:5 + tap + tm, :]
    xs_ref[0:8, :] = xs_ref[tm:tm + 8, :]
    qkv = _silu(y)

    gt = gate_ref[0]
    beta = _sigmoid(gt)
    g = -jnp.exp(alog_ref[...]) * _softplus(gt + dtb_ref[...])

    ri = lax.broadcasted_iota(jnp.int32, (tm, tm), 0)
    ci = lax.broadcasted_iota(jnp.int32, (tm, tm), 1)
    same = (ri // CHUNK) == (ci // CHUNK)
    tril = same & (ci <= ri)
    strict = same & (ci < ri)
    eye = (ri == ci).astype(F32)
    gc = jnp.dot(tril.astype(F32), g, precision=_HI, preferred_element_type=F32)
    gl = jnp.dot(same.astype(F32), g, precision=_HI, preferred_element_type=F32)
    gc_t = gc.T
    row_chunk = lax.broadcasted_iota(jnp.int32, (tm, 1), 0) // CHUNK

    nt = (((1,), (1,)), ((), ()))
    tn = (((0,), (0,)), ((), ()))
    per_head = []
    for h in range(C_HEADS):
        hs = slice(h * C_HEAD_DIM, (h + 1) * C_HEAD_DIM)
        qh = qkv[:, hs]
        kh = qkv[:, C_W + h * C_HEAD_DIM:C_W + (h + 1) * C_HEAD_DIM]
        vh = qkv[:, 2 * C_W + h * C_HEAD_DIM:2 * C_W + (h + 1) * C_HEAD_DIM]
        qh = qh * lax.rsqrt(jnp.sum(qh * qh, axis=-1, keepdims=True) + EPS) * (C_HEAD_DIM ** -0.5)
        kh = kh * lax.rsqrt(jnp.sum(kh * kh, axis=-1, keepdims=True) + EPS)
        b_col = beta[:, BETA_LANE + h:BETA_LANE + h + 1]
        gc_col = gc[:, A_LANE + h:A_LANE + h + 1]
        gl_col = gl[:, A_LANE + h:A_LANE + h + 1]
        gc_row = gc_t[A_LANE + h:A_LANE + h + 1, :]
        kbeta = kh * b_col
        vbeta = vh * b_col
        lm = jnp.where(tril, jnp.exp(jnp.where(tril, gc_col - gc_row, 0.0)), 0.0)
        kb16 = kh.astype(BF16)
        kk = lax.dot_general(kbeta.astype(BF16), kb16, nt, preferred_element_type=F32)
        amat = jnp.where(strict, kk * lm, 0.0)
        qk = lax.dot_general(qh.astype(BF16), kb16, nt, preferred_element_type=F32)
        attn = jnp.where(tril, qk * lm, 0.0)
        tinv = eye
        b = 1
        while b < CHUNK:
            off = ((ri // (2 * b)) == (ci // (2 * b))) & (((ri // b) % 2) == 1) & (((ci // b) % 2) == 0)
            a_off = jnp.where(off, amat, 0.0)
            ta = jnp.dot(tinv, a_off, precision=_HI, preferred_element_type=F32)
            tinv = tinv - jnp.dot(ta, tinv, precision=_HI, preferred_element_type=F32)
            b *= 2
        rhs = jnp.concatenate([vbeta, kbeta * jnp.exp(gc_col)], axis=1)
        uw = jnp.dot(tinv, rhs, precision=_HI, preferred_element_type=F32)
        per_head.append(dict(
            u=uw[:, :C_HEAD_DIM], w=uw[:, C_HEAD_DIM:], attn=attn,
            qg=qh * jnp.exp(gc_col), kd=kh * jnp.exp(gl_col - gc_col),
            decay=jnp.exp(gl_col)))

    for h in range(C_HEADS):
        vn_ref[h] = per_head[h]["u"]
    outs = [[] for _ in range(C_HEADS)]
    for c in range(nchunk):
        rs = slice(c * CHUNK, (c + 1) * CHUNK)
        in_chunk = row_chunk == c
        for h in range(C_HEADS):
            ph = per_head[h]
            s_mat = s_ref[h]
            sb = s_mat.astype(BF16)
            v_new = ph["u"][rs] - jnp.dot(ph["w"][rs].astype(BF16), sb, preferred_element_type=F32)
            vn_ref[h, rs, :] = v_new
            vn_all = vn_ref[h].astype(BF16)
            o = (jnp.dot(ph["qg"][rs].astype(BF16), sb, preferred_element_type=F32)
                 + jnp.dot(ph["attn"][rs].astype(BF16), vn_all, preferred_element_type=F32))
            outs[h].append(o)
            kd = jnp.where(in_chunk, ph["kd"], 0.0).astype(BF16)
            s_ref[h] = s_mat * ph["decay"][c * CHUNK:c * CHUNK + 1] + lax.dot_general(
                kd, vn_all, tn, preferred_element_type=F32)

    for h in range(C_HEADS):
        hs = slice(h * C_HEAD_DIM, (h + 1) * C_HEAD_DIM)
        o = jnp.concatenate(outs[h], axis=0)
        o_ref[0, :, hs] = (_rms(o, gn_ref[...]) * _silu(z_ref[0, :, hs])).astype(BF16)


def _mixer_c(qkv_c, gates, z, conv_w, alog_lane, dtb_lane, gnorm):
    bsz, t, _ = qkv_c.shape
    tm = TM_GDN
    row = lambda b, i: (b, i, 0)
    return pl.pallas_call(
        _gdn_kernel,
        grid=(bsz, t // tm),
        in_specs=[
            pl.BlockSpec((1, tm, C3_W), row),
            pl.BlockSpec((1, tm, GATE_W), row),
            pl.BlockSpec((1, tm, C_W), row),
            _const_spec((C_CONV, C3_W)),
            _const_spec((1, GATE_W)),
            _const_spec((1, GATE_W)),
            _const_spec((1, C_HEAD_DIM)),
        ],
        out_specs=pl.BlockSpec((1, tm, C_W), row),
        out_shape=jax.ShapeDtypeStruct((bsz, t, C_W), BF16),
        scratch_shapes=[
            pltpu.VMEM((tm + 8, C3_W), F32),
            pltpu.VMEM((C_HEADS, C_HEAD_DIM, C_HEAD_DIM), F32),
            pltpu.VMEM((C_HEADS, tm, C_HEAD_DIM), F32),
        ],
        compiler_params=_params(("parallel", "arbitrary")),
        name="mixer_c_gated_deltanet",
    )(qkv_c, gates, z, conv_w, alog_lane, dtb_lane, gnorm)


def _ffn_kernel(x_ref, oa_ref, ob_ref, oc_ref, mod_ref, wo_ref, g_ref, wg_ref, wu_ref,
                cg_ref, cu_ref, wd_ref, fg_ref, o_ref,
                hb_ref, acc_ref, ug_ref, uu_ref, hg_ref, hu_ref, *, final):
    tm = x_ref.shape[1]

    @pl.when(pl.program_id(1) == 0)
    def _():
        hg_ref[...] = jnp.zeros_like(hg_ref)
        hu_ref[...] = jnp.zeros_like(hu_ref)

    gate1 = mod_ref[0, 2:3, :]
    shift2 = mod_ref[0, 3:4, :]
    scale2 = mod_ref[0, 4:5, :]
    gate2 = mod_ref[0, 5:6, :]
    mix = (jnp.dot(oa_ref[0], wo_ref[0:A_W, :], preferred_element_type=F32)
           + jnp.dot(ob_ref[0], wo_ref[A_W:A_W + B_W, :], preferred_element_type=F32)
           + jnp.dot(oc_ref[0], wo_ref[A_W + B_W:, :], preferred_element_type=F32))
    x1 = x_ref[0] + gate1 * mix
    h = _rms(x1, g_ref[...]) * (1.0 + scale2) + shift2
    hb_ref[...] = h.astype(BF16)
    acc_ref[...] = jnp.zeros_like(acc_ref)

    def conv(u_ref, hdr_ref, c, w):
        u_ref[0:8, :] = hdr_ref[c]
        y = (w[0:1, :] * u_ref[6:6 + tm, :] + w[1:2, :] * u_ref[7:7 + tm, :]
             + w[2:3, :] * u_ref[8:8 + tm, :])
        hdr_ref[c] = u_ref[tm:tm + 8, :]
        return y

    def body(c, carry):
        hb = hb_ref[...]
        ug_ref[8:8 + tm, :] = jnp.dot(hb, wg_ref[c], preferred_element_type=F32)
        uu_ref[8:8 + tm, :] = jnp.dot(hb, wu_ref[c], preferred_element_type=F32)
        gate = conv(ug_ref, hg_ref, c, cg_ref[c])
        up = conv(uu_ref, hu_ref, c, cu_ref[c])
        act = (_silu(gate) * up).astype(BF16)
        acc_ref[...] += jnp.dot(act, wd_ref[c], preferred_element_type=F32)
        return carry

    lax.fori_loop(0, N_FF_CH, body, 0)
    x2 = x1 + gate2 * acc_ref[...]
    if final:
        x2 = _rms(x2, fg_ref[...])
    o_ref[0] = x2


def _out_ffn(x, o_a, o_b, o_c, mod_l, w_out, g, wg, wu, cg, cu, wd, final_g, final):
    bsz, t, _ = x.shape
    tm = TM_FFN
    row = lambda b, i: (b, i, 0)
    return pl.pallas_call(
        functools.partial(_ffn_kernel, final=final),
        grid=(bsz, t // tm),
        in_specs=[
            pl.BlockSpec((1, tm, D_MODEL), row),
            pl.BlockSpec((1, tm, A_W), row),
            pl.BlockSpec((1, tm, B_W), row),
            pl.BlockSpec((1, tm, C_W), row),
            pl.BlockSpec((1, 6, D_MODEL), lambda b, i: (b, 0, 0)),
            _const_spec((D_MODEL, D_MODEL)),
            _const_spec((1, D_MODEL)),
            _const_spec((N_FF_CH, D_MODEL, FF_CH)),
            _const_spec((N_FF_CH, D_MODEL, FF_CH)),
            _const_spec((N_FF_CH, FFN_CONV, FF_CH)),
            _const_spec((N_FF_CH, FFN_CONV, FF_CH)),
            _const_spec((N_FF_CH, FF_CH, D_MODEL)),
            _const_spec((1, D_MODEL)),
        ],
        out_specs=pl.BlockSpec((1, tm, D_MODEL), row),
        out_shape=jax.ShapeDtypeStruct((bsz, t, D_MODEL), F32),
        scratch_shapes=[
            pltpu.VMEM((tm, D_MODEL), BF16),
            pltpu.VMEM((tm, D_MODEL), F32),
            pltpu.VMEM((tm + 8, FF_CH), F32),
            pltpu.VMEM((tm + 8, FF_CH), F32),
            pltpu.VMEM((N_FF_CH, 8, FF_CH), F32),
            pltpu.VMEM((N_FF_CH, 8, FF_CH), F32),
        ],
        compiler_params=_params(("parallel", "arbitrary")),
        name="outproj_convmlp",
    )(x, o_a, o_b, o_c, mod_l, w_out, g, wg, wu, cg, cu, wd, final_g)


def _chunk_cols(w, n, width):
    return w.reshape(w.shape[0], n, width).transpose(1, 0, 2)


def _lane_vec(values, start):
    return jnp.zeros((1, GATE_W), F32).at[0, start:start + values.shape[0]].set(values)


def kernel(x, c, ada_w, ada_b, norm_mix_g, norm_ffn_g, w_in, w_out, rel_bias, fgate_bias,
           gdn_conv_w, gdn_A_log, gdn_dt_bias, gdn_norm_g, ffn_w_up, ffn_conv_w,
           ffn_w_down, final_norm_g):
    mod = _modulation(c, ada_w, ada_b)
    final_g = final_norm_g.reshape(1, D_MODEL)
    for l in range(DEPTH):
        qkv_ab, qkv_c, z, gates = _inproj(
            x, mod[l], norm_mix_g[l].reshape(1, D_MODEL), _arrange_w_in(w_in[l]))
        f_rows = _fgate(gates, _lane_vec(fgate_bias[l], F_LANE))
        o_a = _mixer_a(qkv_ab, _rel_bias_tile(rel_bias[l]))
        o_b = _mixer_b(qkv_ab, f_rows)
        o_c = _mixer_c(qkv_c, gates, z, gdn_conv_w[l],
                       _lane_vec(gdn_A_log[l], A_LANE), _lane_vec(gdn_dt_bias[l], A_LANE),
                       gdn_norm_g[l].reshape(1, C_HEAD_DIM))
        w_up = ffn_w_up[l].astype(BF16)
        x = _out_ffn(
            x, o_a, o_b, o_c, mod[l], w_out[l].astype(BF16),
            norm_ffn_g[l].reshape(1, D_MODEL),
            _chunk_cols(w_up[:, :D_FF], N_FF_CH, FF_CH),
            _chunk_cols(w_up[:, D_FF:], N_FF_CH, FF_CH),
            _chunk_cols(ffn_conv_w[l][:, :D_FF], N_FF_CH, FF_CH),
            _chunk_cols(ffn_conv_w[l][:, D_FF:], N_FF_CH, FF_CH),
            ffn_w_down[l].astype(BF16).reshape(N_FF_CH, FF_CH, D_MODEL),
            final_g, final=(l == DEPTH - 1))
    return x
```

```python
import functools

import jax
import jax.numpy as jnp
from jax import lax
from jax.experimental import pallas as pl
from jax.experimental.pallas import tpu as pltpu

F32 = jnp.float32
BF16 = jnp.bfloat16

D_MODEL = 1024
DEPTH = 2
CHUNK = 64
EPS = 1e-6
A_HEADS = 4
A_HEAD_DIM = 64
A_LEFT_CHUNKS = 8
REL_PAST = 256
REL_FUTURE = CHUNK - 1
B_HEADS = 4
B_HEAD_DIM = 64
C_HEADS = 4
C_HEAD_DIM = 128
C_CONV = 4
D_FF = 2816
FFN_CONV = 3
A_W = A_HEADS * A_HEAD_DIM
B_W = B_HEADS * B_HEAD_DIM
C_W = C_HEADS * C_HEAD_DIM
AB_W = 3 * A_W + 3 * B_W
C3_W = 3 * C_W
LANES = 128
GATE_W = LANES
F_LANE, BETA_LANE, A_LANE = 0, 4, 8
NEG = -1e30
VMEM_LIMIT = 56 * 1024 * 1024

TM_PROJ = 512
TF_GATE = 1024
TQ_A = 512
TQ_B = 512
TM_GDN = 256
TM_FFN = 512
FF_CH = 256
N_FF_CH = D_FF // FF_CH


def _sigmoid(x):
    return 1.0 / (1.0 + jnp.exp(-x))


def _silu(x):
    return x * _sigmoid(x)


def _softplus(x):
    return jnp.maximum(x, 0.0) + jnp.log1p(jnp.exp(-jnp.abs(x)))


def _log_sigmoid(x):
    return jnp.minimum(x, 0.0) - jnp.log1p(jnp.exp(-jnp.abs(x)))


def _rms(x, g):
    return x * lax.rsqrt(jnp.mean(x * x, axis=-1, keepdims=True) + EPS) * g


def _params(sem):
    return pltpu.CompilerParams(dimension_semantics=sem, vmem_limit_bytes=VMEM_LIMIT)


def _const_spec(shape):
    nd = len(shape)
    return pl.BlockSpec(shape, lambda *_: (0,) * nd, pipeline_mode=pl.Buffered(1))


def _mod_kernel(c_ref, w_ref, b_ref, o_ref):
    c_act = _silu(c_ref[...])
    o_ref[0] = jnp.dot(c_act, w_ref[0], preferred_element_type=F32) + b_ref[0]


def _modulation(c, ada_w, ada_b):
    bsz = c.shape[0]
    rows = 8
    cp = jnp.zeros((rows, D_MODEL), F32).at[:bsz].set(c)
    ncol = 6 * D_MODEL
    cb = 1536
    out = pl.pallas_call(
        _mod_kernel,
        grid=(DEPTH, ncol // cb),
        in_specs=[
            pl.BlockSpec((rows, D_MODEL), lambda l, j: (0, 0)),
            pl.BlockSpec((1, D_MODEL, cb), lambda l, j: (l, 0, j)),
            pl.BlockSpec((1, 1, cb), lambda l, j: (l, 0, j)),
        ],
        out_specs=pl.BlockSpec((1, rows, cb), lambda l, j: (l, 0, j)),
        out_shape=jax.ShapeDtypeStruct((DEPTH, rows, ncol), F32),
        compiler_params=_params(("arbitrary", "arbitrary")),
        name="adaln_modulation",
    )(cp, ada_w, ada_b.reshape(DEPTH, 1, ncol))
    return out[:, :bsz].reshape(DEPTH, bsz, 6, D_MODEL)


_PROJ_W = AB_W + C3_W + C_W + GATE_W


def _inproj_kernel(x_ref, mod_ref, g_ref, w_ref, oab_ref, oc_ref, oz_ref, og_ref):
    x = x_ref[0]
    shift = mod_ref[0, 0:1, :]
    scale = mod_ref[0, 1:2, :]
    h = _rms(x, g_ref[...]) * (1.0 + scale) + shift
    hb = h.astype(BF16)
    cw = 512
    for j in range(AB_W // cw):
        oab_ref[0, :, j * cw:(j + 1) * cw] = jnp.dot(
            hb, w_ref[:, j * cw:(j + 1) * cw], preferred_element_type=F32).astype(BF16)
    for j in range(C3_W // cw):
        oc_ref[0, :, j * cw:(j + 1) * cw] = jnp.dot(
            hb, w_ref[:, AB_W + j * cw:AB_W + (j + 1) * cw], preferred_element_type=F32)
    oz_ref[0] = jnp.dot(hb, w_ref[:, AB_W + C3_W:AB_W + C3_W + C_W],
                        preferred_element_type=F32)
    og_ref[0] = jnp.dot(hb, w_ref[:, AB_W + C3_W + C_W:], preferred_element_type=F32)


def _inproj(x, mod_l, g, w):
    bsz, t, _ = x.shape
    tm = TM_PROJ
    row = lambda b, i: (b, i, 0)
    return pl.pallas_call(
        _inproj_kernel,
        grid=(bsz, t // tm),
        in_specs=[
            pl.BlockSpec((1, tm, D_MODEL), row),
            pl.BlockSpec((1, 6, D_MODEL), lambda b, i: (b, 0, 0)),
            _const_spec((1, D_MODEL)),
            _const_spec((D_MODEL, _PROJ_W)),
        ],
        out_specs=[
            pl.BlockSpec((1, tm, AB_W), row),
            pl.BlockSpec((1, tm, C3_W), row),
            pl.BlockSpec((1, tm, C_W), row),
            pl.BlockSpec((1, tm, GATE_W), row),
        ],
        out_shape=[
            jax.ShapeDtypeStruct((bsz, t, AB_W), BF16),
            jax.ShapeDtypeStruct((bsz, t, C3_W), F32),
            jax.ShapeDtypeStruct((bsz, t, C_W), F32),
            jax.ShapeDtypeStruct((bsz, t, GATE_W), F32),
        ],
        compiler_params=_params(("parallel", "parallel")),
        name="inproj",
    )(x, mod_l, g, w)


def _arrange_w_in(w_in_l):
    a_end = 3 * A_W
    b_end = a_end + 3 * B_W
    f_end = b_end + B_HEADS
    c_end = f_end + C3_W
    beta_end = c_end + C_HEADS
    a_gate_end = beta_end + C_HEADS
    qa_scale = A_HEAD_DIM ** -0.5
    qb_scale = B_HEAD_DIM ** -0.5
    w_ab = jnp.concatenate([
        w_in_l[:, 0:A_W] * qa_scale, w_in_l[:, A_W:a_end],
        w_in_l[:, a_end:a_end + B_W] * qb_scale, w_in_l[:, a_end + B_W:b_end]], axis=1)
    w_gate = jnp.concatenate([
        w_in_l[:, b_end:f_end], w_in_l[:, c_end:beta_end], w_in_l[:, beta_end:a_gate_end],
        jnp.zeros((D_MODEL, GATE_W - B_HEADS - 2 * C_HEADS), F32)], axis=1)
    w = jnp.concatenate([w_ab, w_in_l[:, f_end:c_end], w_in_l[:, a_gate_end:], w_gate], axis=1)
    return w.astype(BF16)


def _fgate_kernel(g_ref, fb_ref, o_ref, carry_ref):
    @pl.when(pl.program_id(1) == 0)
    def _():
        carry_ref[...] = jnp.zeros_like(carry_ref)

    lf = _log_sigmoid(g_ref[0] + fb_ref[...])
    rows = lf.T[0:8, :]
    tf = rows.shape[1]
    lane = lax.broadcasted_iota(jnp.int32, rows.shape, 1)
    d = 1
    while d < tf:
        rows = rows + jnp.where(lane >= d, pltpu.roll(rows, d, axis=1), 0.0)
        d *= 2
    out = rows + carry_ref[:, 0:1]
    o_ref[0] = out
    carry_ref[...] = jnp.broadcast_to(out[:, tf - 1:tf], carry_ref.shape)


def _fgate(gates, fbias_lane):
    bsz, t, _ = gates.shape
    tf = min(TF_GATE, t)
    return pl.pallas_call(
        _fgate_kernel,
        grid=(bsz, t // tf),
        in_specs=[
            pl.BlockSpec((1, tf, GATE_W), lambda b, i: (b, i, 0)),
            _const_spec((1, GATE_W)),
        ],
        out_specs=pl.BlockSpec((1, 8, tf), lambda b, i: (b, 0, i)),
        out_shape=jax.ShapeDtypeStruct((bsz, 8, t), F32),
        scratch_shapes=[pltpu.VMEM((8, LANES), F32)],
        compiler_params=_params(("parallel", "arbitrary")),
        name="fgate_cumsum",
    )(gates, fbias_lane)


def _mixa_kernel(q_ref, kp_ref, kc_ref, vp_ref, vc_ref, bias_ref, o_ref):
    i = pl.program_id(1)
    tq = q_ref.shape[1]
    lane = lax.broadcasted_iota(jnp.int32, (1, LANES), 1)
    lo = lane < A_HEAD_DIM
    has_prev = i > 0
    nt = (((1,), (1,)), ((), ()))
    for pair in range(A_HEADS // 2):
        sl = slice(pair * LANES, (pair + 1) * LANES)
        q = q_ref[0, :, sl]
        kp, kc = kp_ref[0, :, sl], kc_ref[0, :, sl]
        vp, vc = vp_ref[0, :, sl], vc_ref[0, :, sl]
        outs = []
        for hh in range(2):
            h = 2 * pair + hh
            qh = jnp.where(lo if hh == 0 else ~lo, q, jnp.zeros_like(q))
            s_p = lax.dot_general(qh, kp, nt, preferred_element_type=F32) + bias_ref[h, :, 0:tq]
            s_p = jnp.where(has_prev, s_p, NEG)
            s_c = lax.dot_general(qh, kc, nt, preferred_element_type=F32) + bias_ref[h, :, tq:2 * tq]
            m = jnp.maximum(jnp.max(s_p, axis=-1, keepdims=True),
                            jnp.max(s_c, axis=-1, keepdims=True))
            p_p = jnp.exp(s_p - m)
            p_c = jnp.exp(s_c - m)
            l = jnp.sum(p_p, axis=-1, keepdims=True) + jnp.sum(p_c, axis=-1, keepdims=True)
            o = (jnp.dot(p_p.astype(BF16), vp, preferred_element_type=F32)
                 + jnp.dot(p_c.astype(BF16), vc, preferred_element_type=F32))
            outs.append(o / l)
        o_ref[0, :, sl] = jnp.where(lo, outs[0], outs[1]).astype(BF16)


def _mixer_a(qkv_ab, bias):
    bsz, t, _ = qkv_ab.shape
    tq = TQ_A
    prev = lambda col: (lambda b, i: (b, jnp.maximum(i - 1, 0), col))
    cur = lambda col: (lambda b, i: (b, i, col))
    blk = (1, tq, A_W)
    return pl.pallas_call(
        _mixa_kernel,
        grid=(bsz, t // tq),
        in_specs=[
            pl.BlockSpec(blk, cur(0)),
            pl.BlockSpec(blk, prev(1)), pl.BlockSpec(blk, cur(1)),
            pl.BlockSpec(blk, prev(2)), pl.BlockSpec(blk, cur(2)),
            _const_spec((A_HEADS, tq, 2 * tq)),
        ],
        out_specs=pl.BlockSpec(blk, cur(0)),
        out_shape=jax.ShapeDtypeStruct((bsz, t, A_W), BF16),
        compiler_params=_params(("parallel", "parallel")),
        name="mixer_a_chunk_attn",
    )(qkv_ab, qkv_ab, qkv_ab, qkv_ab, qkv_ab, bias)


_REL_PAD = 384
_REL_ROLL_W = 4 * TQ_A


def _relbias_kernel(tab_ref, o_ref, row_ref):
    tq = o_ref.shape[1]
    m = lax.broadcasted_iota(jnp.int32, (_REL_PAD, _REL_ROLL_W), 1)
    m = jnp.where(m < _REL_ROLL_W // 2, m, m - _REL_ROLL_W)
    idx = jnp.clip(tq - m, -REL_FUTURE, REL_PAST) + REL_FUTURE
    n = lax.broadcasted_iota(jnp.int32, (_REL_PAD, _REL_ROLL_W), 0)
    onehot = (n == idx).astype(F32)
    row_ref[...] = jnp.dot(tab_ref[...], onehot, precision=lax.Precision.HIGHEST,
                           preferred_element_type=F32)
    row = row_ref[pl.ds(pl.program_id(0), 1), :]
    rolled = pltpu.roll(jnp.broadcast_to(row, (tq, _REL_ROLL_W)), 0, 1, stride=1, stride_axis=0)
    qc = lax.broadcasted_iota(jnp.int32, (tq, 2 * tq), 0) // CHUNK
    kc = lax.broadcasted_iota(jnp.int32, (tq, 2 * tq), 1) // CHUNK - tq // CHUNK
    valid = (kc <= qc) & (kc >= qc - A_LEFT_CHUNKS)
    o_ref[0] = jnp.where(valid, rolled[:, 0:2 * tq], NEG)


def _rel_bias_tiles(rel_bias):
    nrel = rel_bias.shape[-1]
    nrow = DEPTH * A_HEADS
    tab = jnp.zeros((nrow, _REL_PAD), F32).at[:, :nrel].set(rel_bias.reshape(nrow, nrel))
    out = pl.pallas_call(
        _relbias_kernel,
        grid=(nrow,),
        in_specs=[pl.BlockSpec((nrow, _REL_PAD), lambda h: (0, 0))],
        out_specs=pl.BlockSpec((1, TQ_A, 2 * TQ_A), lambda h: (h, 0, 0)),
        out_shape=jax.ShapeDtypeStruct((nrow, TQ_A, 2 * TQ_A), F32),
        scratch_shapes=[pltpu.VMEM((nrow, _REL_ROLL_W), F32)],
        compiler_params=_params(("arbitrary",)),
        name="relbias_toeplitz",
    )(tab)
    return out.reshape(DEPTH, A_HEADS, TQ_A, 2 * TQ_A)


def _mixb_kernel(q_ref, k_ref, v_ref, f_ref, o_ref, m_ref, l_ref, acc_ref):
    pair = pl.program_id(1)
    i = pl.program_id(2)
    tq = q_ref.shape[1]
    tk = tq
    lane = lax.broadcasted_iota(jnp.int32, (1, LANES), 1)
    lo = lane < B_HEAD_DIM
    q = q_ref[0]
    qs = (jnp.where(lo, q, jnp.zeros_like(q)), jnp.where(lo, jnp.zeros_like(q), q))
    nt = (((1,), (1,)), ((), ()))
    m_ref[...] = jnp.full(m_ref.shape, NEG, F32)
    l_ref[...] = jnp.zeros(l_ref.shape, F32)
    acc_ref[...] = jnp.zeros(acc_ref.shape, F32)

    def block(j, diagonal):
        off = pl.multiple_of(j * tk, tk)
        k = k_ref[0, pl.ds(off, tk), :]
        v = v_ref[0, pl.ds(off, tk), :]
        for hh in range(2):
            fk = f_ref[0, pl.ds(2 * pair + hh, 1), pl.ds(off, tk)]
            s = lax.dot_general(qs[hh], k, nt, preferred_element_type=F32) - fk
            if diagonal:
                r = lax.broadcasted_iota(jnp.int32, (tq, tk), 0)
                c = lax.broadcasted_iota(jnp.int32, (tq, tk), 1)
                s = jnp.where(c <= r, s, NEG)
            tiles = [s[:, t * LANES:(t + 1) * LANES] for t in range(tk // LANES)]
            m_tile = functools.reduce(jnp.maximum, tiles)
            m_prev = m_ref[hh]
            m_new = jnp.maximum(m_prev, jnp.max(m_tile, axis=-1, keepdims=True))
            alpha = jnp.exp(m_prev - m_new)
            p_tiles = [jnp.exp(t - m_new) for t in tiles]
            l_ref[hh] = alpha * l_ref[hh] + functools.reduce(jnp.add, p_tiles)
            p = jnp.concatenate(p_tiles, axis=1).astype(BF16)
            acc_ref[hh] = alpha * acc_ref[hh] + jnp.dot(p, v, preferred_element_type=F32)
            m_ref[hh] = m_new

    def body(j, carry):
        block(j, False)
        return carry

    lax.fori_loop(0, i, body, 0)
    block(i, True)
    l0 = jnp.sum(l_ref[0], axis=-1, keepdims=True)
    l1 = jnp.sum(l_ref[1], axis=-1, keepdims=True)
    o_ref[0] = jnp.where(lo, acc_ref[0] / l0, acc_ref[1] / l1).astype(BF16)


def _mixer_b(qkv_ab, f_rows):
    bsz, t, _ = qkv_ab.shape
    tq = TQ_B
    base = 3 * A_W // LANES
    npair = B_HEADS // 2
    return pl.pallas_call(
        _mixb_kernel,
        grid=(bsz, npair, t // tq),
        in_specs=[
            pl.BlockSpec((1, tq, LANES), lambda b, p, i: (b, i, base + p)),
            pl.BlockSpec((1, t, LANES), lambda b, p, i: (b, 0, base + npair + p)),
            pl.BlockSpec((1, t, LANES), lambda b, p, i: (b, 0, base + 2 * npair + p)),
            pl.BlockSpec((1, 8, t), lambda b, p, i: (b, 0, 0)),
        ],
        out_specs=pl.BlockSpec((1, tq, LANES), lambda b, p, i: (b, i, p)),
        out_shape=jax.ShapeDtypeStruct((bsz, t, B_W), BF16),
        scratch_shapes=[
            pltpu.VMEM((2, tq, LANES), F32),
            pltpu.VMEM((2, tq, LANES), F32),
            pltpu.VMEM((2, tq, LANES), F32),
        ],
        compiler_params=_params(("parallel", "parallel", "parallel")),
        name="mixer_b_forgetting_attn",
    )(qkv_ab, qkv_ab, qkv_ab, f_rows)


def _gdn_kernel(x_ref, gate_ref, z_ref, cw_ref, alog_ref, dtb_ref, gn_ref, o_ref,
                xs_ref, s_ref, vn_ref):
    tm = x_ref.shape[1]
    nchunk = tm // CHUNK

    @pl.when(pl.program_id(1) == 0)
    def _():
        xs_ref[0:8, :] = jnp.zeros((8, C3_W), F32)
        s_ref[...] = jnp.zeros_like(s_ref)

    xs_ref[8:8 + tm, :] = x_ref[0]
    y = cw_ref[0:1, :] * xs_ref[5:5 + tm, :]
    for tap in range(1, C_CONV):
        y = y + cw_ref[tap:tap + 1, :] * xs_ref[5 + tap:5 + tap + tm, :]
    xs_ref[0:8, :] = xs_ref[tm:tm + 8, :]
    qkv = _silu(y)

    gt = gate_ref[0]
    beta = _sigmoid(gt)
    g = -jnp.exp(alog_ref[...]) * _softplus(gt + dtb_ref[...])

    ri = lax.broadcasted_iota(jnp.int32, (tm, tm), 0)
    ci = lax.broadcasted_iota(jnp.int32, (tm, tm), 1)
    same = (ri // CHUNK) == (ci // CHUNK)
    tril = same & (ci <= ri)
    strict = same & (ci < ri)
    eye = (ri == ci).astype(F32)
    pos = lax.broadcasted_iota(jnp.int32, g.shape, 0) % CHUNK
    gc = g
    d = 1
    while d < CHUNK:
        gc = gc + jnp.where(pos >= d, pltpu.roll(gc, d, axis=0), 0.0)
        d *= 2
    gl = jnp.concatenate(
        [jnp.broadcast_to(gc[(c + 1) * CHUNK - 1:(c + 1) * CHUNK, :], (CHUNK, g.shape[1]))
         for c in range(nchunk)], axis=0)
    gc_t = gc.T
    row_chunk = lax.broadcasted_iota(jnp.int32, (tm, 1), 0) // CHUNK

    nt = (((1,), (1,)), ((), ()))
    tn = (((0,), (0,)), ((), ()))
    per_head = []
    for h in range(C_HEADS):
        hs = slice(h * C_HEAD_DIM, (h + 1) * C_HEAD_DIM)
        qh = qkv[:, hs]
        kh = qkv[:, C_W + h * C_HEAD_DIM:C_W + (h + 1) * C_HEAD_DIM]
        vh = qkv[:, 2 * C_W + h * C_HEAD_DIM:2 * C_W + (h + 1) * C_HEAD_DIM]
        qh = qh * lax.rsqrt(jnp.sum(qh * qh, axis=-1, keepdims=True) + EPS) * (C_HEAD_DIM ** -0.5)
        kh = kh * lax.rsqrt(jnp.sum(kh * kh, axis=-1, keepdims=True) + EPS)
        b_col = beta[:, BETA_LANE + h:BETA_LANE + h + 1]
        gc_col = gc[:, A_LANE + h:A_LANE + h + 1]
        gl_col = gl[:, A_LANE + h:A_LANE + h + 1]
        gc_row = gc_t[A_LANE + h:A_LANE + h + 1, :]
        kbeta = kh * b_col
        vbeta = vh * b_col
        lm = jnp.where(tril, jnp.exp(jnp.where(tril, gc_col - gc_row, 0.0)), 0.0)
        kb16 = kh.astype(BF16)
        kk = lax.dot_general(kbeta.astype(BF16), kb16, nt, preferred_element_type=F32)
        amat = jnp.where(strict, kk * lm, 0.0)
        qk = lax.dot_general(qh.astype(BF16), kb16, nt, preferred_element_type=F32)
        attn = jnp.where(tril, qk * lm, 0.0)
        tinv = eye
        b = 1
        while b < CHUNK:
            off = ((ri // (2 * b)) == (ci // (2 * b))) & (((ri // b) % 2) == 1) & (((ci // b) % 2) == 0)
            a_off = jnp.where(off, amat, 0.0)
            t16 = tinv.astype(BF16)
            ta = jnp.dot(t16, a_off.astype(BF16), preferred_element_type=F32)
            tinv = tinv - jnp.dot(ta.astype(BF16), t16, preferred_element_type=F32)
            b *= 2
        rhs = jnp.concatenate([vbeta, kbeta * jnp.exp(gc_col)], axis=1)
        uw = jnp.dot(tinv.astype(BF16), rhs.astype(BF16), preferred_element_type=F32)
        per_head.append(dict(
            u=uw[:, :C_HEAD_DIM], w=uw[:, C_HEAD_DIM:], attn=attn,
            qg=qh * jnp.exp(gc_col), kd=kh * jnp.exp(gl_col - gc_col),
            decay=jnp.exp(gl_col)))

    for h in range(C_HEADS):
        vn_ref[h] = per_head[h]["u"]
    outs = [[] for _ in range(C_HEADS)]
    for c in range(nchunk):
        rs = slice(c * CHUNK, (c + 1) * CHUNK)
        in_chunk = row_chunk == c
        for h in range(C_HEADS):
            ph = per_head[h]
            s_mat = s_ref[h]
            sb = s_mat.astype(BF16)
            v_new = ph["u"][rs] - jnp.dot(ph["w"][rs].astype(BF16), sb, preferred_element_type=F32)
            vn_ref[h, rs, :] = v_new
            vn_all = vn_ref[h].astype(BF16)
            o = (jnp.dot(ph["qg"][rs].astype(BF16), sb, preferred_element_type=F32)
                 + jnp.dot(ph["attn"][rs].astype(BF16), vn_all, preferred_element_type=F32))
            outs[h].append(o)
            kd = jnp.where(in_chunk, ph["kd"], 0.0).astype(BF16)
            s_ref[h] = s_mat * ph["decay"][c * CHUNK:c * CHUNK + 1] + lax.dot_general(
                kd, vn_all, tn, preferred_element_type=F32)

    for h in range(C_HEADS):
        hs = slice(h * C_HEAD_DIM, (h + 1) * C_HEAD_DIM)
        o = jnp.concatenate(outs[h], axis=0)
        o_ref[0, :, hs] = (_rms(o, gn_ref[...]) * _silu(z_ref[0, :, hs])).astype(BF16)


def _mixer_c(qkv_c, gates, z, conv_w, alog_lane, dtb_lane, gnorm):
    bsz, t, _ = qkv_c.shape
    tm = TM_GDN
    row = lambda b, i: (b, i, 0)
    return pl.pallas_call(
        _gdn_kernel,
        grid=(bsz, t // tm),
        in_specs=[
            pl.BlockSpec((1, tm, C3_W), row),
            pl.BlockSpec((1, tm, GATE_W), row),
            pl.BlockSpec((1, tm, C_W), row),
            _const_spec((C_CONV, C3_W)),
            _const_spec((1, GATE_W)),
            _const_spec((1, GATE_W)),
            _const_spec((1, C_HEAD_DIM)),
        ],
        out_specs=pl.BlockSpec((1, tm, C_W), row),
        out_shape=jax.ShapeDtypeStruct((bsz, t, C_W), BF16),
        scratch_shapes=[
            pltpu.VMEM((tm + 8, C3_W), F32),
            pltpu.VMEM((C_HEADS, C_HEAD_DIM, C_HEAD_DIM), F32),
            pltpu.VMEM((C_HEADS, tm, C_HEAD_DIM), F32),
        ],
        compiler_params=_params(("parallel", "arbitrary")),
        name="mixer_c_gated_deltanet",
    )(qkv_c, gates, z, conv_w, alog_lane, dtb_lane, gnorm)


def _ffn_kernel(x_ref, oa_ref, ob_ref, oc_ref, mod_ref, wo_ref, g_ref, wg_ref, wu_ref,
                cg_ref, cu_ref, wd_ref, fg_ref, o_ref,
                hb_ref, acc_ref, ug_ref, uu_ref, hg_ref, hu_ref, *, final):
    tm = x_ref.shape[1]

    @pl.when(pl.program_id(1) == 0)
    def _():
        hg_ref[...] = jnp.zeros_like(hg_ref)
        hu_ref[...] = jnp.zeros_like(hu_ref)

    gate1 = mod_ref[0, 2:3, :]
    shift2 = mod_ref[0, 3:4, :]
    scale2 = mod_ref[0, 4:5, :]
    gate2 = mod_ref[0, 5:6, :]
    mix = (jnp.dot(oa_ref[0], wo_ref[0:A_W, :], preferred_element_type=F32)
           + jnp.dot(ob_ref[0], wo_ref[A_W:A_W + B_W, :], preferred_element_type=F32)
           + jnp.dot(oc_ref[0], wo_ref[A_W + B_W:, :], preferred_element_type=F32))
    x1 = x_ref[0] + gate1 * mix
    h = _rms(x1, g_ref[...]) * (1.0 + scale2) + shift2
    hb_ref[...] = h.astype(BF16)
    acc_ref[...] = jnp.zeros_like(acc_ref)

    def conv(u_ref, hdr_ref, c, w):
        u_ref[0:8, :] = hdr_ref[c]
        y = (w[0:1, :] * u_ref[6:6 + tm, :] + w[1:2, :] * u_ref[7:7 + tm, :]
             + w[2:3, :] * u_ref[8:8 + tm, :])
        hdr_ref[c] = u_ref[tm:tm + 8, :]
        return y

    def body(c, carry):
        hb = hb_ref[...]
        ug_ref[8:8 + tm, :] = jnp.dot(hb, wg_ref[c], preferred_element_type=F32)
        uu_ref[8:8 + tm, :] = jnp.dot(hb, wu_ref[c], preferred_element_type=F32)
        gate = conv(ug_ref, hg_ref, c, cg_ref[c])
        up = conv(uu_ref, hu_ref, c, cu_ref[c])
        act = (_silu(gate) * up).astype(BF16)
        acc_ref[...] += jnp.dot(act, wd_ref[c], preferred_element_type=F32)
        return carry

    lax.fori_loop(0, N_FF_CH, body, 0)
    x2 = x1 + gate2 * acc_ref[...]
    if final:
        x2 = _rms(x2, fg_ref[...])
    o_ref[0] = x2


def _out_ffn(x, o_a, o_b, o_c, mod_l, w_out, g, wg, wu, cg, cu, wd, final_g, final):
    bsz, t, _ = x.shape
    tm = TM_FFN
    row = lambda b, i: (b, i, 0)
    return pl.pallas_call(
        functools.partial(_ffn_kernel, final=final),
        grid=(bsz, t // tm),
        in_specs=[
            pl.BlockSpec((1, tm, D_MODEL), row),
            pl.BlockSpec((1, tm, A_W), row),
            pl.BlockSpec((1, tm, B_W), row),
            pl.BlockSpec((1, tm, C_W), row),
            pl.BlockSpec((1, 6, D_MODEL), lambda b, i: (b, 0, 0)),
            _const_spec((D_MODEL, D_MODEL)),
            _const_spec((1, D_MODEL)),
            _const_spec((N_FF_CH, D_MODEL, FF_CH)),
            _const_spec((N_FF_CH, D_MODEL, FF_CH)),
            _const_spec((N_FF_CH, FFN_CONV, FF_CH)),
            _const_spec((N_FF_CH, FFN_CONV, FF_CH)),
            _const_spec((N_FF_CH, FF_CH, D_MODEL)),
            _const_spec((1, D_MODEL)),
        ],
        out_specs=pl.BlockSpec((1, tm, D_MODEL), row),
        out_shape=jax.ShapeDtypeStruct((bsz, t, D_MODEL), F32),
        scratch_shapes=[
            pltpu.VMEM((tm, D_MODEL), BF16),
            pltpu.VMEM((tm, D_MODEL), F32),
            pltpu.VMEM((tm + 8, FF_CH), F32),
            pltpu.VMEM((tm + 8, FF_CH), F32),
            pltpu.VMEM((N_FF_CH, 8, FF_CH), F32),
            pltpu.VMEM((N_FF_CH, 8, FF_CH), F32),
        ],
        compiler_params=_params(("parallel", "arbitrary")),
        name="outproj_convmlp",
    )(x, o_a, o_b, o_c, mod_l, w_out, g, wg, wu, cg, cu, wd, final_g)


def _chunk_cols(w, n, width):
    return w.reshape(w.shape[0], n, width).transpose(1, 0, 2)


def _lane_vec(values, start):
    return jnp.zeros((1, GATE_W), F32).at[0, start:start + values.shape[0]].set(values)


def kernel(x, c, ada_w, ada_b, norm_mix_g, norm_ffn_g, w_in, w_out, rel_bias, fgate_bias,
           gdn_conv_w, gdn_A_log, gdn_dt_bias, gdn_norm_g, ffn_w_up, ffn_conv_w,
           ffn_w_down, final_norm_g):
    mod = _modulation(c, ada_w, ada_b)
    final_g = final_norm_g.reshape(1, D_MODEL)
    bias_tiles = _rel_bias_tiles(rel_bias)
    for l in range(DEPTH):
        qkv_ab, qkv_c, z, gates = _inproj(
            x, mod[l], norm_mix_g[l].reshape(1, D_MODEL), _arrange_w_in(w_in[l]))
        f_rows = _fgate(gates, _lane_vec(fgate_bias[l], F_LANE))
        o_a = _mixer_a(qkv_ab, bias_tiles[l])
        o_b = _mixer_b(qkv_ab, f_rows)
        o_c = _mixer_c(qkv_c, gates, z, gdn_conv_w[l],
                       _lane_vec(gdn_A_log[l], A_LANE), _lane_vec(gdn_dt_bias[l], A_LANE),
                       gdn_norm_g[l].reshape(1, C_HEAD_DIM))
        w_up = ffn_w_up[l].astype(BF16)
        x = _out_ffn(
            x, o_a, o_b, o_c, mod[l], w_out[l].astype(BF16),
            norm_ffn_g[l].reshape(1, D_MODEL),
            _chunk_cols(w_up[:, :D_FF], N_FF_CH, FF_CH),
            _chunk_cols(w_up[:, D_FF:], N_FF_CH, FF_CH),
            _chunk_cols(ffn_conv_w[l][:, :D_FF], N_FF_CH, FF_CH),
            _chunk_cols(ffn_conv_w[l][:, D_FF:], N_FF_CH, FF_CH),
            ffn_w_down[l].astype(BF16).reshape(N_FF_CH, FF_CH, D_MODEL),
            final_g, final=(l == DEPTH - 1))
    return x
```

```python
import functools

import jax
import jax.numpy as jnp
from jax import lax
from jax.experimental import pallas as pl
from jax.experimental.pallas import tpu as pltpu

F32 = jnp.float32
BF16 = jnp.bfloat16

D_MODEL = 1024
DEPTH = 2
CHUNK = 64
EPS = 1e-6
A_HEADS = 4
A_HEAD_DIM = 64
A_LEFT_CHUNKS = 8
REL_PAST = 256
REL_FUTURE = CHUNK - 1
B_HEADS = 4
B_HEAD_DIM = 64
C_HEADS = 4
C_HEAD_DIM = 128
C_CONV = 4
D_FF = 2816
FFN_CONV = 3
A_W = A_HEADS * A_HEAD_DIM
B_W = B_HEADS * B_HEAD_DIM
C_W = C_HEADS * C_HEAD_DIM
AB_W = 3 * A_W + 3 * B_W
C3_W = 3 * C_W
LANES = 128
GATE_W = LANES
F_LANE, BETA_LANE, A_LANE = 0, 4, 8
NEG = -1e30
LOG2E = 1.4426950408889634
F_PARTS = 3
VMEM_LIMIT = 56 * 1024 * 1024

TM_PROJ = 512
TF_GATE = 1024
TQ_A = 512
TQ_B = 512
TM_GDN = 256
SB_GDN = 128
NB_GDN = 2
TM_FFN = 512
FF_CH = 256
N_FF_CH = D_FF // FF_CH


def _sigmoid(x):
    return 1.0 / (1.0 + jnp.exp(-x))


def _silu(x):
    return x * _sigmoid(x)


def _softplus(x):
    return jnp.maximum(x, 0.0) + jnp.log1p(jnp.exp(-jnp.abs(x)))


def _log_sigmoid(x):
    return jnp.minimum(x, 0.0) - jnp.log1p(jnp.exp(-jnp.abs(x)))


def _rms(x, g):
    return x * lax.rsqrt(jnp.mean(x * x, axis=-1, keepdims=True) + EPS) * g


def _params(sem):
    return pltpu.CompilerParams(dimension_semantics=sem, vmem_limit_bytes=VMEM_LIMIT)


def _const_spec(shape):
    nd = len(shape)
    return pl.BlockSpec(shape, lambda *_: (0,) * nd, pipeline_mode=pl.Buffered(1))


def _mod_kernel(c_ref, w_ref, b_ref, o_ref):
    c_act = _silu(c_ref[...])
    o_ref[0] = jnp.dot(c_act, w_ref[0], preferred_element_type=F32) + b_ref[0]


def _modulation(c, ada_w, ada_b):
    bsz = c.shape[0]
    rows = 8
    cp = jnp.zeros((rows, D_MODEL), F32).at[:bsz].set(c)
    ncol = 6 * D_MODEL
    cb = 1536
    out = pl.pallas_call(
        _mod_kernel,
        grid=(DEPTH, ncol // cb),
        in_specs=[
            pl.BlockSpec((rows, D_MODEL), lambda l, j: (0, 0)),
            pl.BlockSpec((1, D_MODEL, cb), lambda l, j: (l, 0, j)),
            pl.BlockSpec((1, 1, cb), lambda l, j: (l, 0, j)),
        ],
        out_specs=pl.BlockSpec((1, rows, cb), lambda l, j: (l, 0, j)),
        out_shape=jax.ShapeDtypeStruct((DEPTH, rows, ncol), F32),
        compiler_params=_params(("arbitrary", "arbitrary")),
        name="adaln_modulation",
    )(cp, ada_w, ada_b.reshape(DEPTH, 1, ncol))
    return out[:, :bsz].reshape(DEPTH, bsz, 6, D_MODEL)


_PROJ_W = AB_W + C3_W + C_W + GATE_W


def _inproj_kernel(x_ref, mod_ref, g_ref, w_ref, oab_ref, oc_ref, oz_ref, og_ref):
    x = x_ref[0]
    shift = mod_ref[0, 0:1, :]
    scale = mod_ref[0, 1:2, :]
    h = _rms(x, g_ref[...]) * (1.0 + scale) + shift
    hb = h.astype(BF16)
    cw = 512
    for j in range(AB_W // cw):
        oab_ref[0, :, j * cw:(j + 1) * cw] = jnp.dot(
            hb, w_ref[:, j * cw:(j + 1) * cw], preferred_element_type=F32).astype(BF16)
    for j in range(C3_W // cw):
        oc_ref[0, :, j * cw:(j + 1) * cw] = jnp.dot(
            hb, w_ref[:, AB_W + j * cw:AB_W + (j + 1) * cw], preferred_element_type=F32)
    oz_ref[0] = jnp.dot(hb, w_ref[:, AB_W + C3_W:AB_W + C3_W + C_W],
                        preferred_element_type=F32)
    og_ref[0] = jnp.dot(hb, w_ref[:, AB_W + C3_W + C_W:], preferred_element_type=F32)


def _inproj(x, mod_l, g, w):
    bsz, t, _ = x.shape
    tm = TM_PROJ
    row = lambda b, i: (b, i, 0)
    return pl.pallas_call(
        _inproj_kernel,
        grid=(bsz, t // tm),
        in_specs=[
            pl.BlockSpec((1, tm, D_MODEL), row),
            pl.BlockSpec((1, 6, D_MODEL), lambda b, i: (b, 0, 0)),
            _const_spec((1, D_MODEL)),
            _const_spec((D_MODEL, _PROJ_W)),
        ],
        out_specs=[
            pl.BlockSpec((1, tm, AB_W), row),
            pl.BlockSpec((1, tm, C3_W), row),
            pl.BlockSpec((1, tm, C_W), row),
            pl.BlockSpec((1, tm, GATE_W), row),
        ],
        out_shape=[
            jax.ShapeDtypeStruct((bsz, t, AB_W), BF16),
            jax.ShapeDtypeStruct((bsz, t, C3_W), F32),
            jax.ShapeDtypeStruct((bsz, t, C_W), F32),
            jax.ShapeDtypeStruct((bsz, t, GATE_W), F32),
        ],
        compiler_params=_params(("parallel", "parallel")),
        name="inproj",
    )(x, mod_l, g, w)


def _arrange_w_in(w_in_l):
    a_end = 3 * A_W
    b_end = a_end + 3 * B_W
    f_end = b_end + B_HEADS
    c_end = f_end + C3_W
    beta_end = c_end + C_HEADS
    a_gate_end = beta_end + C_HEADS
    qa_scale = A_HEAD_DIM ** -0.5
    qb_scale = B_HEAD_DIM ** -0.5 * LOG2E
    w_ab = jnp.concatenate([
        w_in_l[:, 0:A_W] * qa_scale, w_in_l[:, A_W:a_end],
        w_in_l[:, a_end:a_end + B_W] * qb_scale, w_in_l[:, a_end + B_W:b_end]], axis=1)
    w_gate = jnp.concatenate([
        w_in_l[:, b_end:f_end], w_in_l[:, c_end:beta_end], w_in_l[:, beta_end:a_gate_end],
        jnp.zeros((D_MODEL, GATE_W - B_HEADS - 2 * C_HEADS), F32)], axis=1)
    w = jnp.concatenate([w_ab, w_in_l[:, f_end:c_end], w_in_l[:, a_gate_end:], w_gate], axis=1)
    return w.astype(BF16)


def _fgate_kernel(g_ref, fb_ref, o_ref, carry_ref):
    @pl.when(pl.program_id(1) == 0)
    def _():
        carry_ref[...] = jnp.zeros_like(carry_ref)

    tf = g_ref.shape[1]
    lane = lax.broadcasted_iota(jnp.int32, (tf, GATE_W), 1)
    row = lax.broadcasted_iota(jnp.int32, (tf, GATE_W), 0)
    cs = jnp.where(lane < B_HEADS, _log_sigmoid(g_ref[0] + fb_ref[...]), 0.0)
    d = 1
    while d < tf:
        cs = cs + jnp.where(row >= d, pltpu.roll(cs, d, axis=0), 0.0)
        d *= 2
    cs = cs + carry_ref[0:1, :]
    carry_ref[...] = jnp.broadcast_to(cs[tf - 1:tf, :], carry_ref.shape)
    rest = cs * (-LOG2E)
    out = jnp.zeros_like(rest)
    for i in range(F_PARTS):
        part = rest.astype(BF16).astype(F32)
        rest = rest - part
        out = out + (part if i == 0 else pltpu.roll(part, i * B_HEADS, axis=1))
    o_ref[0] = out.astype(BF16)


def _fgate(gates, fbias_lane):
    bsz, t, _ = gates.shape
    tf = min(TF_GATE, t)
    return pl.pallas_call(
        _fgate_kernel,
        grid=(bsz, t // tf),
        in_specs=[
            pl.BlockSpec((1, tf, GATE_W), lambda b, i: (b, i, 0)),
            _const_spec((1, GATE_W)),
        ],
        out_specs=pl.BlockSpec((1, tf, GATE_W), lambda b, i: (b, i, 0)),
        out_shape=jax.ShapeDtypeStruct((bsz, t, GATE_W), BF16),
        scratch_shapes=[pltpu.VMEM((8, LANES), F32)],
        compiler_params=_params(("parallel", "arbitrary")),
        name="fgate_cumsum",
    )(gates, fbias_lane)


def _mixa_kernel(q_ref, kp_ref, kc_ref, vp_ref, vc_ref, bias_ref, o_ref):
    i = pl.program_id(1)
    tq = q_ref.shape[1]
    lane = lax.broadcasted_iota(jnp.int32, (1, LANES), 1)
    lo = lane < A_HEAD_DIM
    has_prev = i > 0
    nt = (((1,), (1,)), ((), ()))
    for pair in range(A_HEADS // 2):
        sl = slice(pair * LANES, (pair + 1) * LANES)
        q = q_ref[0, :, sl]
        kp, kc = kp_ref[0, :, sl], kc_ref[0, :, sl]
        vp, vc = vp_ref[0, :, sl], vc_ref[0, :, sl]
        outs = []
        for hh in range(2):
            h = 2 * pair + hh
            qh = jnp.where(lo if hh == 0 else ~lo, q, jnp.zeros_like(q))
            s_p = lax.dot_general(qh, kp, nt, preferred_element_type=F32) + bias_ref[h, :, 0:tq]
            s_p = jnp.where(has_prev, s_p, NEG)
            s_c = lax.dot_general(qh, kc, nt, preferred_element_type=F32) + bias_ref[h, :, tq:2 * tq]
            m = jnp.maximum(jnp.max(s_p, axis=-1, keepdims=True),
                            jnp.max(s_c, axis=-1, keepdims=True))
            p_p = jnp.exp(s_p - m)
            p_c = jnp.exp(s_c - m)
            l = jnp.sum(p_p, axis=-1, keepdims=True) + jnp.sum(p_c, axis=-1, keepdims=True)
            o = (jnp.dot(p_p.astype(BF16), vp, preferred_element_type=F32)
                 + jnp.dot(p_c.astype(BF16), vc, preferred_element_type=F32))
            outs.append(o / l)
        o_ref[0, :, sl] = jnp.where(lo, outs[0], outs[1]).astype(BF16)


def _mixer_a(qkv_ab, bias):
    bsz, t, _ = qkv_ab.shape
    tq = TQ_A
    prev = lambda col: (lambda b, i: (b, jnp.maximum(i - 1, 0), col))
    cur = lambda col: (lambda b, i: (b, i, col))
    blk = (1, tq, A_W)
    return pl.pallas_call(
        _mixa_kernel,
        grid=(bsz, t // tq),
        in_specs=[
            pl.BlockSpec(blk, cur(0)),
            pl.BlockSpec(blk, prev(1)), pl.BlockSpec(blk, cur(1)),
            pl.BlockSpec(blk, prev(2)), pl.BlockSpec(blk, cur(2)),
            _const_spec((A_HEADS, tq, 2 * tq)),
        ],
        out_specs=pl.BlockSpec(blk, cur(0)),
        out_shape=jax.ShapeDtypeStruct((bsz, t, A_W), BF16),
        compiler_params=_params(("parallel", "parallel")),
        name="mixer_a_chunk_attn",
    )(qkv_ab, qkv_ab, qkv_ab, qkv_ab, qkv_ab, bias)


_REL_PAD = 384
_REL_ROLL_W = 4 * TQ_A


def _relbias_kernel(tab_ref, o_ref, row_ref):
    tq = o_ref.shape[1]
    m = lax.broadcasted_iota(jnp.int32, (_REL_PAD, _REL_ROLL_W), 1)
    m = jnp.where(m < _REL_ROLL_W // 2, m, m - _REL_ROLL_W)
    idx = jnp.clip(tq - m, -REL_FUTURE, REL_PAST) + REL_FUTURE
    n = lax.broadcasted_iota(jnp.int32, (_REL_PAD, _REL_ROLL_W), 0)
    onehot = (n == idx).astype(F32)
    row_ref[...] = jnp.dot(tab_ref[...], onehot, precision=lax.Precision.HIGHEST,
                           preferred_element_type=F32)
    row = row_ref[pl.ds(pl.program_id(0), 1), :]
    rolled = pltpu.roll(jnp.broadcast_to(row, (tq, _REL_ROLL_W)), 0, 1, stride=1, stride_axis=0)
    qc = lax.broadcasted_iota(jnp.int32, (tq, 2 * tq), 0) // CHUNK
    kc = lax.broadcasted_iota(jnp.int32, (tq, 2 * tq), 1) // CHUNK - tq // CHUNK
    valid = (kc <= qc) & (kc >= qc - A_LEFT_CHUNKS)
    o_ref[0] = jnp.where(valid, rolled[:, 0:2 * tq], NEG)


def _rel_bias_tiles(rel_bias):
    nrel = rel_bias.shape[-1]
    nrow = DEPTH * A_HEADS
    tab = jnp.zeros((nrow, _REL_PAD), F32).at[:, :nrel].set(rel_bias.reshape(nrow, nrel))
    out = pl.pallas_call(
        _relbias_kernel,
        grid=(nrow,),
        in_specs=[pl.BlockSpec((nrow, _REL_PAD), lambda h: (0, 0))],
        out_specs=pl.BlockSpec((1, TQ_A, 2 * TQ_A), lambda h: (h, 0, 0)),
        out_shape=jax.ShapeDtypeStruct((nrow, TQ_A, 2 * TQ_A), F32),
        scratch_shapes=[pltpu.VMEM((nrow, _REL_ROLL_W), F32)],
        compiler_params=_params(("arbitrary",)),
        name="relbias_toeplitz",
    )(tab)
    return out.reshape(DEPTH, A_HEADS, TQ_A, 2 * TQ_A)


def _mixb_kernel(q_ref, k_ref, v_ref, f_ref, o_ref, s_ref, m_ref, l_ref, acc_ref):
    pair = pl.program_id(1)
    i = pl.program_id(2)
    tq = q_ref.shape[1]
    tk = tq
    lane = lax.broadcasted_iota(jnp.int32, (1, LANES), 1)
    lo = lane < B_HEAD_DIM
    q = q_ref[0]
    zero = jnp.zeros_like(q)
    qs = []
    for hh in range(2):
        head = 2 * pair + hh
        picks = functools.reduce(jnp.logical_or, [lane == head + t * B_HEADS for t in range(F_PARTS)])
        ones = jnp.broadcast_to(jnp.where(picks, 1.0, 0.0).astype(F32), (tq, LANES)).astype(BF16)
        qs.append(jnp.concatenate([jnp.where(lo if hh == 0 else ~lo, q, zero), ones], axis=1))
    nt = (((1,), (1,)), ((), ()))
    m_ref[...] = jnp.full(m_ref.shape, NEG, F32)
    l_ref[...] = jnp.zeros(l_ref.shape, F32)
    acc_ref[...] = jnp.zeros(acc_ref.shape, F32)

    def logits(j):
        off = pl.multiple_of(j * tk, tk)
        kf = jnp.concatenate([k_ref[0, pl.ds(off, tk), :], f_ref[0, pl.ds(off, tk), :]], axis=1)
        return [lax.dot_general(qs[hh], kf, nt, preferred_element_type=F32) for hh in range(2)]

    def consume(j, diagonal):
        off = pl.multiple_of(j * tk, tk)
        v = v_ref[0, pl.ds(off, tk), :]
        for hh in range(2):
            s = s_ref[hh]
            if diagonal:
                r = lax.broadcasted_iota(jnp.int32, (tq, tk), 0)
                c = lax.broadcasted_iota(jnp.int32, (tq, tk), 1)
                s = jnp.where(c <= r, s, NEG)
            tiles = [s[:, t * LANES:(t + 1) * LANES] for t in range(tk // LANES)]
            m_tile = functools.reduce(jnp.maximum, tiles)
            m_prev = m_ref[hh]
            m_new = jnp.maximum(m_prev, jnp.max(m_tile, axis=-1, keepdims=True))
            alpha = jnp.exp2(m_prev - m_new)
            p_tiles = [jnp.exp2(t - m_new) for t in tiles]
            l_ref[hh] = alpha * l_ref[hh] + functools.reduce(jnp.add, p_tiles)
            p = jnp.concatenate(p_tiles, axis=1).astype(BF16)
            acc_ref[hh] = alpha * acc_ref[hh] + jnp.dot(p, v, preferred_element_type=F32)
            m_ref[hh] = m_new

    def store(s_pair):
        for hh in range(2):
            s_ref[hh] = s_pair[hh]

    store(logits(0))

    def body(j, carry):
        nxt = logits(j + 1)
        consume(j, False)
        store(nxt)
        return carry

    lax.fori_loop(0, i, body, 0)
    consume(i, True)
    l0 = jnp.sum(l_ref[0], axis=-1, keepdims=True)
    l1 = jnp.sum(l_ref[1], axis=-1, keepdims=True)
    o_ref[0] = jnp.where(lo, acc_ref[0] / l0, acc_ref[1] / l1).astype(BF16)


def _mixer_b(qkv_ab, f_cols):
    bsz, t, _ = qkv_ab.shape
    tq = TQ_B
    base = 3 * A_W // LANES
    npair = B_HEADS // 2
    return pl.pallas_call(
        _mixb_kernel,
        grid=(bsz, npair, t // tq),
        in_specs=[
            pl.BlockSpec((1, tq, LANES), lambda b, p, i: (b, i, base + p)),
            pl.BlockSpec((1, t, LANES), lambda b, p, i: (b, 0, base + npair + p)),
            pl.BlockSpec((1, t, LANES), lambda b, p, i: (b, 0, base + 2 * npair + p)),
            pl.BlockSpec((1, t, GATE_W), lambda b, p, i: (b, 0, 0)),
        ],
        out_specs=pl.BlockSpec((1, tq, LANES), lambda b, p, i: (b, i, p)),
        out_shape=jax.ShapeDtypeStruct((bsz, t, B_W), BF16),
        scratch_shapes=[
            pltpu.VMEM((2, tq, tq), F32),
            pltpu.VMEM((2, tq, LANES), F32),
            pltpu.VMEM((2, tq, LANES), F32),
            pltpu.VMEM((2, tq, LANES), F32),
        ],
        compiler_params=_params(("parallel", "parallel", "parallel")),
        name="mixer_b_forgetting_attn",
    )(qkv_ab, qkv_ab, qkv_ab, f_cols)


def _gdn_kernel(x_ref, gate_ref, z_ref, cw_ref, alog_ref, dtb_ref, gn_ref, o_ref,
                xs_ref, s_ref, vn_ref):
    nb, tm = x_ref.shape[0], x_ref.shape[1]
    sb = SB_GDN
    nsub = tm // sb
    cps = sb // CHUNK

    @pl.when(pl.program_id(1) == 0)
    def _():
        xs_ref[:, 0:8, :] = jnp.zeros((nb, 8, C3_W), F32)
        s_ref[...] = jnp.zeros_like(s_ref)

    ri = lax.broadcasted_iota(jnp.int32, (sb, sb), 0)
    ci = lax.broadcasted_iota(jnp.int32, (sb, sb), 1)
    same = (ri // CHUNK) == (ci // CHUNK)
    tril = same & (ci <= ri)
    strict = same & (ci < ri)
    eye = (ri == ci).astype(F32)

    def level_mask(b):
        return ((ri // (2 * b)) == (ci // (2 * b))) & (((ri // b) % 2) == 1) & (((ci // b) % 2) == 0)

    pos = lax.broadcasted_iota(jnp.int32, (tm, GATE_W), 0) % CHUNK
    sub_chunk = lax.broadcasted_iota(jnp.int32, (sb, 1), 0) // CHUNK
    nt = (((1,), (1,)), ((), ()))
    tn = (((0,), (0,)), ((), ()))

    units = [(bb, sub, h) for bb in range(nb) for sub in range(nsub) for h in range(C_HEADS)]
    gates = {}
    qkvs = {}
    for bb in range(nb):
        xs = xs_ref.at[bb]
        xs[8:8 + tm, :] = x_ref[bb]
        y = cw_ref[0:1, :] * xs[5:5 + tm, :]
        for tap in range(1, C_CONV):
            y = y + cw_ref[tap:tap + 1, :] * xs[5 + tap:5 + tap + tm, :]
        xs[0:8, :] = xs[tm:tm + 8, :]
        qkvs[bb] = _silu(y)

        gt = gate_ref[bb]
        beta = _sigmoid(gt)
        g = -jnp.exp(alog_ref[...]) * _softplus(gt + dtb_ref[...])
        gc = g
        d = 1
        while d < CHUNK:
            gc = gc + jnp.where(pos >= d, pltpu.roll(gc, d, axis=0), 0.0)
            d *= 2
        gl = jnp.concatenate(
            [jnp.broadcast_to(gc[(c + 1) * CHUNK - 1:(c + 1) * CHUNK, :], (CHUNK, GATE_W))
             for c in range(tm // CHUNK)], axis=0)
        gates[bb] = (beta, gc, gl, gc.T)

    work = {}
    for u in units:
        bb, sub, h = u
        rows = slice(sub * sb, (sub + 1) * sb)
        qkv = qkvs[bb]
        beta, gc, gl, gc_t = gates[bb]
        qh = qkv[rows, h * C_HEAD_DIM:(h + 1) * C_HEAD_DIM]
        kh = qkv[rows, C_W + h * C_HEAD_DIM:C_W + (h + 1) * C_HEAD_DIM]
        vh = qkv[rows, 2 * C_W + h * C_HEAD_DIM:2 * C_W + (h + 1) * C_HEAD_DIM]
        qh = qh * lax.rsqrt(jnp.sum(qh * qh, axis=-1, keepdims=True) + EPS) * (C_HEAD_DIM ** -0.5)
        kh = kh * lax.rsqrt(jnp.sum(kh * kh, axis=-1, keepdims=True) + EPS)
        b_col = beta[rows, BETA_LANE + h:BETA_LANE + h + 1]
        gc_col = gc[rows, A_LANE + h:A_LANE + h + 1]
        gl_col = gl[rows, A_LANE + h:A_LANE + h + 1]
        gc_row = gc_t[A_LANE + h:A_LANE + h + 1, rows]
        kbeta = kh * b_col
        lm = jnp.where(tril, jnp.exp(jnp.where(tril, gc_col - gc_row, 0.0)), 0.0)
        kq = lax.dot_general(jnp.concatenate([kbeta, qh], axis=0).astype(BF16), kh.astype(BF16),
                             nt, preferred_element_type=F32)
        amat = jnp.where(strict, kq[0:sb] * lm, 0.0)
        work[u] = dict(
            amat=amat, attn=jnp.where(tril, kq[sb:2 * sb] * lm, 0.0),
            rhs=jnp.concatenate([vh * b_col, kbeta * jnp.exp(gc_col)], axis=1).astype(BF16),
            qg=qh * jnp.exp(gc_col), kd=kh * jnp.exp(gl_col - gc_col), decay=jnp.exp(gl_col),
            tinv=eye - jnp.where(level_mask(1), amat, 0.0))

    b = 2
    while b < CHUNK:
        mask = level_mask(b)
        tas = {}
        for u in units:
            t16 = work[u]["tinv"].astype(BF16)
            work[u]["t16"] = t16
            tas[u] = jnp.dot(t16, jnp.where(mask, work[u]["amat"], 0.0).astype(BF16),
                             preferred_element_type=F32)
        for u in units:
            work[u]["tinv"] = work[u]["tinv"] - jnp.dot(tas[u].astype(BF16), work[u]["t16"],
                                                        preferred_element_type=F32)
        b *= 2
    for u in units:
        uw = jnp.dot(work[u]["tinv"].astype(BF16), work[u]["rhs"], preferred_element_type=F32)
        work[u]["u"] = uw[:, :C_HEAD_DIM]
        work[u]["w"] = uw[:, C_HEAD_DIM:]

    chains = [(bb, h) for bb in range(nb) for h in range(C_HEADS)]
    outs = {ch: [] for ch in chains}
    for sub in range(nsub):
        for bb, h in chains:
            vn_ref[bb * C_HEADS + h] = work[bb, sub, h]["u"]
        for cc in range(cps):
            rs = slice(cc * CHUNK, (cc + 1) * CHUNK)
            in_chunk = sub_chunk == cc
            first = {}
            for bb, h in chains:
                ph = work[bb, sub, h]
                s_mat = s_ref[bb, h]
                first[bb, h] = (s_mat, jnp.dot(
                    jnp.concatenate([ph["w"][rs], ph["qg"][rs]], axis=0).astype(BF16),
                    s_mat.astype(BF16), preferred_element_type=F32))
            for bb, h in chains:
                ph = work[bb, sub, h]
                s_mat, ws = first[bb, h]
                vn = vn_ref.at[bb * C_HEADS + h]
                vn[rs, :] = ph["u"][rs] - ws[0:CHUNK]
                vn_all = vn[...].astype(BF16)
                outs[bb, h].append(ws[CHUNK:] + jnp.dot(ph["attn"][rs].astype(BF16), vn_all,
                                                        preferred_element_type=F32))
                kd = jnp.where(in_chunk, ph["kd"], 0.0).astype(BF16)
                s_ref[bb, h] = s_mat * ph["decay"][cc * CHUNK:cc * CHUNK + 1] + lax.dot_general(
                    kd, vn_all, tn, preferred_element_type=F32)

    for bb in range(nb):
        for h in range(C_HEADS):
            hs = slice(h * C_HEAD_DIM, (h + 1) * C_HEAD_DIM)
            o = jnp.concatenate(outs[bb, h], axis=0)
            o_ref[bb, :, hs] = (_rms(o, gn_ref[...]) * _silu(z_ref[bb, :, hs])).astype(BF16)


def _mixer_c(qkv_c, gates, z, conv_w, alog_lane, dtb_lane, gnorm):
    bsz, t, _ = qkv_c.shape
    tm = TM_GDN
    nb = NB_GDN if bsz % NB_GDN == 0 else 1
    row = lambda b, i: (b, i, 0)
    return pl.pallas_call(
        _gdn_kernel,
        grid=(bsz // nb, t // tm),
        in_specs=[
            pl.BlockSpec((nb, tm, C3_W), row),
            pl.BlockSpec((nb, tm, GATE_W), row),
            pl.BlockSpec((nb, tm, C_W), row),
            _const_spec((C_CONV, C3_W)),
            _const_spec((1, GATE_W)),
            _const_spec((1, GATE_W)),
            _const_spec((1, C_HEAD_DIM)),
        ],
        out_specs=pl.BlockSpec((nb, tm, C_W), row),
        out_shape=jax.ShapeDtypeStruct((bsz, t, C_W), BF16),
        scratch_shapes=[
            pltpu.VMEM((nb, tm + 8, C3_W), F32),
            pltpu.VMEM((nb, C_HEADS, C_HEAD_DIM, C_HEAD_DIM), F32),
            pltpu.VMEM((nb * C_HEADS, SB_GDN, C_HEAD_DIM), F32),
        ],
        compiler_params=_params(("parallel", "arbitrary")),
        name="mixer_c_gated_deltanet",
    )(qkv_c, gates, z, conv_w, alog_lane, dtb_lane, gnorm)


def _ffn_kernel(x_ref, oa_ref, ob_ref, oc_ref, mod_ref, wo_ref, g_ref, wg_ref, wu_ref,
                cg_ref, cu_ref, wd_ref, fg_ref, o_ref,
                hb_ref, acc_ref, ug_ref, uu_ref, hg_ref, hu_ref, *, final):
    tm = x_ref.shape[1]

    @pl.when(pl.program_id(1) == 0)
    def _():
        hg_ref[...] = jnp.zeros_like(hg_ref)
        hu_ref[...] = jnp.zeros_like(hu_ref)

    gate1 = mod_ref[0, 2:3, :]
    shift2 = mod_ref[0, 3:4, :]
    scale2 = mod_ref[0, 4:5, :]
    gate2 = mod_ref[0, 5:6, :]
    mix = (jnp.dot(oa_ref[0], wo_ref[0:A_W, :], preferred_element_type=F32)
           + jnp.dot(ob_ref[0], wo_ref[A_W:A_W + B_W, :], preferred_element_type=F32)
           + jnp.dot(oc_ref[0], wo_ref[A_W + B_W:, :], preferred_element_type=F32))
    x1 = x_ref[0] + gate1 * mix
    h = _rms(x1, g_ref[...]) * (1.0 + scale2) + shift2
    hb_ref[...] = h.astype(BF16)
    acc_ref[...] = jnp.zeros_like(acc_ref)

    def up_proj(c):
        hb = hb_ref[...]
        ug_ref[c % 2, 8:8 + tm, :] = jnp.dot(hb, wg_ref[c], preferred_element_type=F32)
        uu_ref[c % 2, 8:8 + tm, :] = jnp.dot(hb, wu_ref[c], preferred_element_type=F32)

    def conv(u_ref, hdr_ref, c, w):
        u = u_ref.at[c % 2]
        u[0:8, :] = hdr_ref[c]
        y = w[0:1, :] * u[6:6 + tm, :] + w[1:2, :] * u[7:7 + tm, :] + w[2:3, :] * u[8:8 + tm, :]
        hdr_ref[c] = u[tm:tm + 8, :]
        return y

    up_proj(0)
    for c in range(N_FF_CH):
        if c + 1 < N_FF_CH:
            up_proj(c + 1)
        gate = conv(ug_ref, hg_ref, c, cg_ref[c])
        up = conv(uu_ref, hu_ref, c, cu_ref[c])
        act = (_silu(gate) * up).astype(BF16)
        acc_ref[...] += jnp.dot(act, wd_ref[c], preferred_element_type=F32)
    x2 = x1 + gate2 * acc_ref[...]
    if final:
        x2 = _rms(x2, fg_ref[...])
    o_ref[0] = x2


def _out_ffn(x, o_a, o_b, o_c, mod_l, w_out, g, wg, wu, cg, cu, wd, final_g, final):
    bsz, t, _ = x.shape
    tm = TM_FFN
    row = lambda b, i: (b, i, 0)
    return pl.pallas_call(
        functools.partial(_ffn_kernel, final=final),
        grid=(bsz, t // tm),
        in_specs=[
            pl.BlockSpec((1, tm, D_MODEL), row),
            pl.BlockSpec((1, tm, A_W), row),
            pl.BlockSpec((1, tm, B_W), row),
            pl.BlockSpec((1, tm, C_W), row),
            pl.BlockSpec((1, 6, D_MODEL), lambda b, i: (b, 0, 0)),
            _const_spec((D_MODEL, D_MODEL)),
            _const_spec((1, D_MODEL)),
            _const_spec((N_FF_CH, D_MODEL, FF_CH)),
            _const_spec((N_FF_CH, D_MODEL, FF_CH)),
            _const_spec((N_FF_CH, FFN_CONV, FF_CH)),
            _const_spec((N_FF_CH, FFN_CONV, FF_CH)),
            _const_spec((N_FF_CH, FF_CH, D_MODEL)),
            _const_spec((1, D_MODEL)),
        ],
        out_specs=pl.BlockSpec((1, tm, D_MODEL), row),
        out_shape=jax.ShapeDtypeStruct((bsz, t, D_MODEL), F32),
        scratch_shapes=[
            pltpu.VMEM((tm, D_MODEL), BF16),
            pltpu.VMEM((tm, D_MODEL), F32),
            pltpu.VMEM((2, tm + 8, FF_CH), F32),
            pltpu.VMEM((2, tm + 8, FF_CH), F32),
            pltpu.VMEM((N_FF_CH, 8, FF_CH), F32),
            pltpu.VMEM((N_FF_CH, 8, FF_CH), F32),
        ],
        compiler_params=_params(("parallel", "arbitrary")),
        name="outproj_convmlp",
    )(x, o_a, o_b, o_c, mod_l, w_out, g, wg, wu, cg, cu, wd, final_g)


def _chunk_cols(w, n, width):
    return w.reshape(w.shape[0], n, width).transpose(1, 0, 2)


def _lane_vec(values, start):
    return jnp.zeros((1, GATE_W), F32).at[0, start:start + values.shape[0]].set(values)


def kernel(x, c, ada_w, ada_b, norm_mix_g, norm_ffn_g, w_in, w_out, rel_bias, fgate_bias,
           gdn_conv_w, gdn_A_log, gdn_dt_bias, gdn_norm_g, ffn_w_up, ffn_conv_w,
           ffn_w_down, final_norm_g):
    mod = _modulation(c, ada_w, ada_b)
    final_g = final_norm_g.reshape(1, D_MODEL)
    bias_tiles = _rel_bias_tiles(rel_bias)
    for l in range(DEPTH):
        qkv_ab, qkv_c, z, gates = _inproj(
            x, mod[l], norm_mix_g[l].reshape(1, D_MODEL), _arrange_w_in(w_in[l]))
        f_cols = _fgate(gates, _lane_vec(fgate_bias[l], F_LANE))
        o_a = _mixer_a(qkv_ab, bias_tiles[l])
        o_b = _mixer_b(qkv_ab, f_cols)
        o_c = _mixer_c(qkv_c, gates, z, gdn_conv_w[l],
                       _lane_vec(gdn_A_log[l], A_LANE), _lane_vec(gdn_dt_bias[l], A_LANE),
                       gdn_norm_g[l].reshape(1, C_HEAD_DIM))
        w_up = ffn_w_up[l].astype(BF16)
        x = _out_ffn(
            x, o_a, o_b, o_c, mod[l], w_out[l].astype(BF16),
            norm_ffn_g[l].reshape(1, D_MODEL),
            _chunk_cols(w_up[:, :D_FF], N_FF_CH, FF_CH),
            _chunk_cols(w_up[:, D_FF:], N_FF_CH, FF_CH),
            _chunk_cols(ffn_conv_w[l][:, :D_FF], N_FF_CH, FF_CH),
            _chunk_cols(ffn_conv_w[l][:, D_FF:], N_FF_CH, FF_CH),
            ffn_w_down[l].astype(BF16).reshape(N_FF_CH, FF_CH, D_MODEL),
            final_g, final=(l == DEPTH - 1))
    return x
```

```python
import functools

import jax
import jax.numpy as jnp
from jax import lax
from jax.experimental import pallas as pl
from jax.experimental.pallas import tpu as pltpu

F32 = jnp.float32
BF16 = jnp.bfloat16

D_MODEL = 1024
DEPTH = 2
CHUNK = 64
EPS = 1e-6
A_HEADS = 4
A_HEAD_DIM = 64
A_LEFT_CHUNKS = 8
REL_PAST = 256
REL_FUTURE = CHUNK - 1
B_HEADS = 4
B_HEAD_DIM = 64
C_HEADS = 4
C_HEAD_DIM = 128
C_CONV = 4
D_FF = 2816
FFN_CONV = 3
A_W = A_HEADS * A_HEAD_DIM
B_W = B_HEADS * B_HEAD_DIM
C_W = C_HEADS * C_HEAD_DIM
AB_W = 3 * A_W + 3 * B_W
C3_W = 3 * C_W
LANES = 128
GATE_W = LANES
F_LANE, BETA_LANE, A_LANE = 0, 4, 8
NEG = -1e30
LOG2E = 1.4426950408889634
F_PARTS = 3
VMEM_LIMIT = 56 * 1024 * 1024

TM_PROJ = 512
TF_GATE = 1024
TQ_A = 512
TQ_B = 512
TM_GDN = 256
SB_GDN = 128
NB_GDN = 2
TM_FFN = 512
FF_CH = 256
N_FF_CH = D_FF // FF_CH


def _sigmoid(x):
    return 1.0 / (1.0 + jnp.exp(-x))


def _silu(x):
    return x * _sigmoid(x)


def _softplus(x):
    return jnp.maximum(x, 0.0) + jnp.log1p(jnp.exp(-jnp.abs(x)))


def _log_sigmoid(x):
    return jnp.minimum(x, 0.0) - jnp.log1p(jnp.exp(-jnp.abs(x)))


def _rms(x, g):
    return x * lax.rsqrt(jnp.mean(x * x, axis=-1, keepdims=True) + EPS) * g


def _params(sem):
    return pltpu.CompilerParams(dimension_semantics=sem, vmem_limit_bytes=VMEM_LIMIT)


def _const_spec(shape):
    nd = len(shape)
    return pl.BlockSpec(shape, lambda *_: (0,) * nd, pipeline_mode=pl.Buffered(1))


def _mod_kernel(c_ref, w_ref, b_ref, o_ref):
    c_act = _silu(c_ref[...])
    o_ref[0] = jnp.dot(c_act, w_ref[0], preferred_element_type=F32) + b_ref[0]


def _modulation(c, ada_w, ada_b):
    bsz = c.shape[0]
    rows = 8
    cp = jnp.zeros((rows, D_MODEL), F32).at[:bsz].set(c)
    ncol = 6 * D_MODEL
    cb = 1536
    out = pl.pallas_call(
        _mod_kernel,
        grid=(DEPTH, ncol // cb),
        in_specs=[
            pl.BlockSpec((rows, D_MODEL), lambda l, j: (0, 0)),
            pl.BlockSpec((1, D_MODEL, cb), lambda l, j: (l, 0, j)),
            pl.BlockSpec((1, 1, cb), lambda l, j: (l, 0, j)),
        ],
        out_specs=pl.BlockSpec((1, rows, cb), lambda l, j: (l, 0, j)),
        out_shape=jax.ShapeDtypeStruct((DEPTH, rows, ncol), F32),
        compiler_params=_params(("arbitrary", "arbitrary")),
        name="adaln_modulation",
    )(cp, ada_w, ada_b.reshape(DEPTH, 1, ncol))
    return out[:, :bsz].reshape(DEPTH, bsz, 6, D_MODEL)


_PROJ_W = AB_W + C3_W + C_W + GATE_W


def _inproj_kernel(x_ref, mod_ref, g_ref, w_ref, oab_ref, oc_ref, oz_ref, og_ref):
    x = x_ref[0]
    shift = mod_ref[0, 0:1, :]
    scale = mod_ref[0, 1:2, :]
    h = _rms(x, g_ref[...]) * (1.0 + scale) + shift
    hb = h.astype(BF16)
    cw = 512
    for j in range(AB_W // cw):
        oab_ref[0, :, j * cw:(j + 1) * cw] = jnp.dot(
            hb, w_ref[:, j * cw:(j + 1) * cw], preferred_element_type=F32).astype(BF16)
    for j in range(C3_W // cw):
        oc_ref[0, :, j * cw:(j + 1) * cw] = jnp.dot(
            hb, w_ref[:, AB_W + j * cw:AB_W + (j + 1) * cw], preferred_element_type=F32)
    oz_ref[0] = jnp.dot(hb, w_ref[:, AB_W + C3_W:AB_W + C3_W + C_W],
                        preferred_element_type=F32)
    og_ref[0] = jnp.dot(hb, w_ref[:, AB_W + C3_W + C_W:], preferred_element_type=F32)


def _inproj(x, mod_l, g, w):
    bsz, t, _ = x.shape
    tm = TM_PROJ
    row = lambda b, i: (b, i, 0)
    return pl.pallas_call(
        _inproj_kernel,
        grid=(bsz, t // tm),
        in_specs=[
            pl.BlockSpec((1, tm, D_MODEL), row),
            pl.BlockSpec((1, 6, D_MODEL), lambda b, i: (b, 0, 0)),
            _const_spec((1, D_MODEL)),
            _const_spec((D_MODEL, _PROJ_W)),
        ],
        out_specs=[
            pl.BlockSpec((1, tm, AB_W), row),
            pl.BlockSpec((1, tm, C3_W), row),
            pl.BlockSpec((1, tm, C_W), row),
            pl.BlockSpec((1, tm, GATE_W), row),
        ],
        out_shape=[
            jax.ShapeDtypeStruct((bsz, t, AB_W), BF16),
            jax.ShapeDtypeStruct((bsz, t, C3_W), F32),
            jax.ShapeDtypeStruct((bsz, t, C_W), F32),
            jax.ShapeDtypeStruct((bsz, t, GATE_W), F32),
        ],
        compiler_params=_params(("parallel", "parallel")),
        name="inproj",
    )(x, mod_l, g, w)


def _arrange_w_in(w_in_l):
    a_end = 3 * A_W
    b_end = a_end + 3 * B_W
    f_end = b_end + B_HEADS
    c_end = f_end + C3_W
    beta_end = c_end + C_HEADS
    a_gate_end = beta_end + C_HEADS
    qa_scale = A_HEAD_DIM ** -0.5
    qb_scale = B_HEAD_DIM ** -0.5 * LOG2E
    w_ab = jnp.concatenate([
        w_in_l[:, 0:A_W] * qa_scale, w_in_l[:, A_W:a_end],
        w_in_l[:, a_end:a_end + B_W] * qb_scale, w_in_l[:, a_end + B_W:b_end]], axis=1)
    w_gate = jnp.concatenate([
        w_in_l[:, b_end:f_end], w_in_l[:, c_end:beta_end], w_in_l[:, beta_end:a_gate_end],
        jnp.zeros((D_MODEL, GATE_W - B_HEADS - 2 * C_HEADS), F32)], axis=1)
    w = jnp.concatenate([w_ab, w_in_l[:, f_end:c_end], w_in_l[:, a_gate_end:], w_gate], axis=1)
    return w.astype(BF16)


def _fgate_kernel(g_ref, fb_ref, o_ref, carry_ref):
    @pl.when(pl.program_id(1) == 0)
    def _():
        carry_ref[...] = jnp.zeros_like(carry_ref)

    tf = g_ref.shape[1]
    lane = lax.broadcasted_iota(jnp.int32, (tf, GATE_W), 1)
    row = lax.broadcasted_iota(jnp.int32, (tf, GATE_W), 0)
    cs = jnp.where(lane < B_HEADS, _log_sigmoid(g_ref[0] + fb_ref[...]), 0.0)
    d = 1
    while d < tf:
        cs = cs + jnp.where(row >= d, pltpu.roll(cs, d, axis=0), 0.0)
        d *= 2
    cs = cs + carry_ref[0:1, :]
    carry_ref[...] = jnp.broadcast_to(cs[tf - 1:tf, :], carry_ref.shape)
    rest = cs * (-LOG2E)
    out = jnp.zeros_like(rest)
    for i in range(F_PARTS):
        part = rest.astype(BF16).astype(F32)
        rest = rest - part
        out = out + (part if i == 0 else pltpu.roll(part, i * B_HEADS, axis=1))
    o_ref[0] = out.astype(BF16)


def _fgate(gates, fbias_lane):
    bsz, t, _ = gates.shape
    tf = min(TF_GATE, t)
    return pl.pallas_call(
        _fgate_kernel,
        grid=(bsz, t // tf),
        in_specs=[
            pl.BlockSpec((1, tf, GATE_W), lambda b, i: (b, i, 0)),
            _const_spec((1, GATE_W)),
        ],
        out_specs=pl.BlockSpec((1, tf, GATE_W), lambda b, i: (b, i, 0)),
        out_shape=jax.ShapeDtypeStruct((bsz, t, GATE_W), BF16),
        scratch_shapes=[pltpu.VMEM((8, LANES), F32)],
        compiler_params=_params(("parallel", "arbitrary")),
        name="fgate_cumsum",
    )(gates, fbias_lane)


def _mixa_kernel(q_ref, kp_ref, kc_ref, vp_ref, vc_ref, bias_ref, o_ref):
    i = pl.program_id(1)
    tq = q_ref.shape[1]
    lane = lax.broadcasted_iota(jnp.int32, (1, LANES), 1)
    lo = lane < A_HEAD_DIM
    has_prev = i > 0
    nt = (((1,), (1,)), ((), ()))
    logits = []
    for h in range(A_HEADS):
        sl = slice((h // 2) * LANES, (h // 2 + 1) * LANES)
        q = q_ref[0, :, sl]
        qh = jnp.where(lo if h % 2 == 0 else ~lo, q, jnp.zeros_like(q))
        logits.append((lax.dot_general(qh, kp_ref[0, :, sl], nt, preferred_element_type=F32),
                       lax.dot_general(qh, kc_ref[0, :, sl], nt, preferred_element_type=F32)))
    outs = []
    for h in range(A_HEADS):
        sl = slice((h // 2) * LANES, (h // 2 + 1) * LANES)
        s_p = jnp.where(has_prev, logits[h][0] + bias_ref[h, :, 0:tq], NEG)
        s_c = logits[h][1] + bias_ref[h, :, tq:2 * tq]
        m = jnp.maximum(jnp.max(s_p, axis=-1, keepdims=True), jnp.max(s_c, axis=-1, keepdims=True))
        p_p = jnp.exp(s_p - m)
        p_c = jnp.exp(s_c - m)
        l = jnp.sum(p_p, axis=-1, keepdims=True) + jnp.sum(p_c, axis=-1, keepdims=True)
        o = (jnp.dot(p_p.astype(BF16), vp_ref[0, :, sl], preferred_element_type=F32)
             + jnp.dot(p_c.astype(BF16), vc_ref[0, :, sl], preferred_element_type=F32))
        outs.append(o / l)
    for pair in range(A_HEADS // 2):
        o_ref[0, :, pair * LANES:(pair + 1) * LANES] = jnp.where(
            lo, outs[2 * pair], outs[2 * pair + 1]).astype(BF16)


def _mixer_a(qkv_ab, bias):
    bsz, t, _ = qkv_ab.shape
    tq = TQ_A
    prev = lambda col: (lambda b, i: (b, jnp.maximum(i - 1, 0), col))
    cur = lambda col: (lambda b, i: (b, i, col))
    blk = (1, tq, A_W)
    return pl.pallas_call(
        _mixa_kernel,
        grid=(bsz, t // tq),
        in_specs=[
            pl.BlockSpec(blk, cur(0)),
            pl.BlockSpec(blk, prev(1)), pl.BlockSpec(blk, cur(1)),
            pl.BlockSpec(blk, prev(2)), pl.BlockSpec(blk, cur(2)),
            _const_spec((A_HEADS, tq, 2 * tq)),
        ],
        out_specs=pl.BlockSpec(blk, cur(0)),
        out_shape=jax.ShapeDtypeStruct((bsz, t, A_W), BF16),
        compiler_params=_params(("parallel", "parallel")),
        name="mixer_a_chunk_attn",
    )(qkv_ab, qkv_ab, qkv_ab, qkv_ab, qkv_ab, bias)


_REL_PAD = 384
_REL_ROLL_W = 4 * TQ_A


def _relbias_kernel(tab_ref, o_ref, row_ref):
    tq = o_ref.shape[1]
    m = lax.broadcasted_iota(jnp.int32, (_REL_PAD, _REL_ROLL_W), 1)
    m = jnp.where(m < _REL_ROLL_W // 2, m, m - _REL_ROLL_W)
    idx = jnp.clip(tq - m, -REL_FUTURE, REL_PAST) + REL_FUTURE
    n = lax.broadcasted_iota(jnp.int32, (_REL_PAD, _REL_ROLL_W), 0)
    onehot = (n == idx).astype(F32)
    row_ref[...] = jnp.dot(tab_ref[...], onehot, precision=lax.Precision.HIGHEST,
                           preferred_element_type=F32)
    row = row_ref[pl.ds(pl.program_id(0), 1), :]
    rolled = pltpu.roll(jnp.broadcast_to(row, (tq, _REL_ROLL_W)), 0, 1, stride=1, stride_axis=0)
    qc = lax.broadcasted_iota(jnp.int32, (tq, 2 * tq), 0) // CHUNK
    kc = lax.broadcasted_iota(jnp.int32, (tq, 2 * tq), 1) // CHUNK - tq // CHUNK
    valid = (kc <= qc) & (kc >= qc - A_LEFT_CHUNKS)
    o_ref[0] = jnp.where(valid, rolled[:, 0:2 * tq], NEG)


def _rel_bias_tiles(rel_bias):
    nrel = rel_bias.shape[-1]
    nrow = DEPTH * A_HEADS
    tab = jnp.zeros((nrow, _REL_PAD), F32).at[:, :nrel].set(rel_bias.reshape(nrow, nrel))
    out = pl.pallas_call(
        _relbias_kernel,
        grid=(nrow,),
        in_specs=[pl.BlockSpec((nrow, _REL_PAD), lambda h: (0, 0))],
        out_specs=pl.BlockSpec((1, TQ_A, 2 * TQ_A), lambda h: (h, 0, 0)),
        out_shape=jax.ShapeDtypeStruct((nrow, TQ_A, 2 * TQ_A), F32),
        scratch_shapes=[pltpu.VMEM((nrow, _REL_ROLL_W), F32)],
        compiler_params=_params(("arbitrary",)),
        name="relbias_toeplitz",
    )(tab)
    return out.reshape(DEPTH, A_HEADS, TQ_A, 2 * TQ_A)


def _mixb_kernel(q_ref, k_ref, v_ref, f_ref, o_ref, s_ref, p_ref, m_ref, l_ref, acc_ref):
    pair = pl.program_id(1)
    i = pl.program_id(2)
    tq = q_ref.shape[1]
    tk = tq
    lane = lax.broadcasted_iota(jnp.int32, (1, LANES), 1)
    lo = lane < B_HEAD_DIM
    q = q_ref[0]
    zero = jnp.zeros_like(q)
    qs = []
    for hh in range(2):
        head = 2 * pair + hh
        picks = functools.reduce(jnp.logical_or, [lane == head + t * B_HEADS for t in range(F_PARTS)])
        ones = jnp.broadcast_to(jnp.where(picks, 1.0, 0.0).astype(F32), (tq, LANES)).astype(BF16)
        qs.append(jnp.concatenate([jnp.where(lo if hh == 0 else ~lo, q, zero), ones], axis=1))
    nt = (((1,), (1,)), ((), ()))
    m_ref[...] = jnp.full(m_ref.shape, NEG, F32)
    l_ref[...] = jnp.zeros(l_ref.shape, F32)
    acc_ref[...] = jnp.zeros(acc_ref.shape, F32)
    p_ref[...] = jnp.zeros(p_ref.shape, BF16)

    def logits(j):
        off = pl.multiple_of(j * tk, tk)
        kf = jnp.concatenate([k_ref[0, pl.ds(off, tk), :], f_ref[0, pl.ds(off, tk), :]], axis=1)
        return [lax.dot_general(qs[hh], kf, nt, preferred_element_type=F32) for hh in range(2)]

    def pending_pv(j):
        off = pl.multiple_of(jnp.maximum(j - 1, 0) * tk, tk)
        v = v_ref[0, pl.ds(off, tk), :]
        return [jnp.dot(p_ref[hh], v, preferred_element_type=F32) for hh in range(2)]

    def softmax(hh, diagonal):
        s = s_ref[hh]
        if diagonal:
            r = lax.broadcasted_iota(jnp.int32, (tq, tk), 0)
            c = lax.broadcasted_iota(jnp.int32, (tq, tk), 1)
            s = jnp.where(c <= r, s, NEG)
        tiles = [s[:, t * LANES:(t + 1) * LANES] for t in range(tk // LANES)]
        m_tile = functools.reduce(jnp.maximum, tiles)
        m_prev = m_ref[hh]
        m_new = jnp.maximum(m_prev, jnp.max(m_tile, axis=-1, keepdims=True))
        alpha = jnp.exp2(m_prev - m_new)
        p_tiles = [jnp.exp2(t - m_new) for t in tiles]
        l_ref[hh] = alpha * l_ref[hh] + functools.reduce(jnp.add, p_tiles)
        m_ref[hh] = m_new
        return alpha, jnp.concatenate(p_tiles, axis=1).astype(BF16)

    def store(s_pair):
        for hh in range(2):
            s_ref[hh] = s_pair[hh]

    store(logits(0))

    def body(j, carry):
        pv = pending_pv(j)
        nxt = logits(j + 1)
        for hh in range(2):
            alpha, p = softmax(hh, False)
            p_ref[hh] = p
            acc_ref[hh] = alpha * (acc_ref[hh] + pv[hh])
        store(nxt)
        return carry

    lax.fori_loop(0, i, body, 0)
    pv = pending_pv(i)
    v = v_ref[0, pl.ds(pl.multiple_of(i * tk, tk), tk), :]
    outs = []
    for hh in range(2):
        alpha, p = softmax(hh, True)
        acc = alpha * (acc_ref[hh] + pv[hh]) + jnp.dot(p, v, preferred_element_type=F32)
        outs.append(acc / jnp.sum(l_ref[hh], axis=-1, keepdims=True))
    o_ref[0] = jnp.where(lo, outs[0], outs[1]).astype(BF16)


def _mixer_b(qkv_ab, f_cols):
    bsz, t, _ = qkv_ab.shape
    tq = TQ_B
    base = 3 * A_W // LANES
    npair = B_HEADS // 2
    return pl.pallas_call(
        _mixb_kernel,
        grid=(bsz, npair, t // tq),
        in_specs=[
            pl.BlockSpec((1, tq, LANES), lambda b, p, i: (b, i, base + p)),
            pl.BlockSpec((1, t, LANES), lambda b, p, i: (b, 0, base + npair + p)),
            pl.BlockSpec((1, t, LANES), lambda b, p, i: (b, 0, base + 2 * npair + p)),
            pl.BlockSpec((1, t, GATE_W), lambda b, p, i: (b, 0, 0)),
        ],
        out_specs=pl.BlockSpec((1, tq, LANES), lambda b, p, i: (b, i, p)),
        out_shape=jax.ShapeDtypeStruct((bsz, t, B_W), BF16),
        scratch_shapes=[
            pltpu.VMEM((2, tq, tq), F32),
            pltpu.VMEM((2, tq, tq), BF16),
            pltpu.VMEM((2, tq, LANES), F32),
            pltpu.VMEM((2, tq, LANES), F32),
            pltpu.VMEM((2, tq, LANES), F32),
        ],
        compiler_params=_params(("parallel", "parallel", "parallel")),
        name="mixer_b_forgetting_attn",
    )(qkv_ab, qkv_ab, qkv_ab, f_cols)


def _gdn_kernel(x_ref, gate_ref, z_ref, cw_ref, alog_ref, dtb_ref, gn_ref, o_ref,
                xs_ref, s_ref, vn_ref):
    nb, tm = x_ref.shape[0], x_ref.shape[1]
    sb = SB_GDN
    nsub = tm // sb
    cps = sb // CHUNK

    @pl.when(pl.program_id(1) == 0)
    def _():
        xs_ref[:, 0:8, :] = jnp.zeros((nb, 8, C3_W), F32)
        s_ref[...] = jnp.zeros_like(s_ref)

    ri = lax.broadcasted_iota(jnp.int32, (sb, sb), 0)
    ci = lax.broadcasted_iota(jnp.int32, (sb, sb), 1)
    same = (ri // CHUNK) == (ci // CHUNK)
    tril = same & (ci <= ri)
    strict = same & (ci < ri)
    eye = (ri == ci).astype(F32)

    def level_mask(b):
        return ((ri // (2 * b)) == (ci // (2 * b))) & (((ri // b) % 2) == 1) & (((ci // b) % 2) == 0)

    pos = lax.broadcasted_iota(jnp.int32, (tm, GATE_W), 0) % CHUNK
    sub_chunk = lax.broadcasted_iota(jnp.int32, (sb, 1), 0) // CHUNK
    nt = (((1,), (1,)), ((), ()))
    tn = (((0,), (0,)), ((), ()))

    units = [(bb, sub, h) for bb in range(nb) for sub in range(nsub) for h in range(C_HEADS)]
    gates = {}
    qkvs = {}
    for bb in range(nb):
        xs = xs_ref.at[bb]
        xs[8:8 + tm, :] = x_ref[bb]
        y = cw_ref[0:1, :] * xs[5:5 + tm, :]
        for tap in range(1, C_CONV):
            y = y + cw_ref[tap:tap + 1, :] * xs[5 + tap:5 + tap + tm, :]
        xs[0:8, :] = xs[tm:tm + 8, :]
        qkvs[bb] = _silu(y)

        gt = gate_ref[bb]
        beta = _sigmoid(gt)
        g = -jnp.exp(alog_ref[...]) * _softplus(gt + dtb_ref[...])
        gc = g
        d = 1
        while d < CHUNK:
            gc = gc + jnp.where(pos >= d, pltpu.roll(gc, d, axis=0), 0.0)
            d *= 2
        gl = jnp.concatenate(
            [jnp.broadcast_to(gc[(c + 1) * CHUNK - 1:(c + 1) * CHUNK, :], (CHUNK, GATE_W))
             for c in range(tm // CHUNK)], axis=0)
        gates[bb] = (beta, gc, gl, gc.T)

    work = {}
    for u in units:
        bb, sub, h = u
        rows = slice(sub * sb, (sub + 1) * sb)
        qkv = qkvs[bb]
        beta, gc, gl, gc_t = gates[bb]
        qh = qkv[rows, h * C_HEAD_DIM:(h + 1) * C_HEAD_DIM]
        kh = qkv[rows, C_W + h * C_HEAD_DIM:C_W + (h + 1) * C_HEAD_DIM]
        vh = qkv[rows, 2 * C_W + h * C_HEAD_DIM:2 * C_W + (h + 1) * C_HEAD_DIM]
        qh = qh * lax.rsqrt(jnp.sum(qh * qh, axis=-1, keepdims=True) + EPS) * (C_HEAD_DIM ** -0.5)
        kh = kh * lax.rsqrt(jnp.sum(kh * kh, axis=-1, keepdims=True) + EPS)
        b_col = beta[rows, BETA_LANE + h:BETA_LANE + h + 1]
        gc_col = gc[rows, A_LANE + h:A_LANE + h + 1]
        gl_col = gl[rows, A_LANE + h:A_LANE + h + 1]
        gc_row = gc_t[A_LANE + h:A_LANE + h + 1, rows]
        kbeta = kh * b_col
        lm = jnp.where(tril, jnp.exp(jnp.where(tril, gc_col - gc_row, 0.0)), 0.0)
        kq = lax.dot_general(jnp.concatenate([kbeta, qh], axis=0).astype(BF16), kh.astype(BF16),
                             nt, preferred_element_type=F32)
        amat = jnp.where(strict, kq[0:sb] * lm, 0.0)
        work[u] = dict(
            amat=amat, attn=jnp.where(tril, kq[sb:2 * sb] * lm, 0.0),
            rhs=jnp.concatenate([vh * b_col, kbeta * jnp.exp(gc_col)], axis=1).astype(BF16),
            qg=qh * jnp.exp(gc_col), kd=kh * jnp.exp(gl_col - gc_col), decay=jnp.exp(gl_col),
            tinv=eye - jnp.where(level_mask(1), amat, 0.0))

    b = 2
    while b < CHUNK:
        mask = level_mask(b)
        tas = {}
        for u in units:
            t16 = work[u]["tinv"].astype(BF16)
            work[u]["t16"] = t16
            tas[u] = jnp.dot(t16, jnp.where(mask, work[u]["amat"], 0.0).astype(BF16),
                             preferred_element_type=F32)
        for u in units:
            work[u]["tinv"] = work[u]["tinv"] - jnp.dot(tas[u].astype(BF16), work[u]["t16"],
                                                        preferred_element_type=F32)
        b *= 2
    for u in units:
        uw = jnp.dot(work[u]["tinv"].astype(BF16), work[u]["rhs"], preferred_element_type=F32)
        work[u]["u"] = uw[:, :C_HEAD_DIM]
        work[u]["w"] = uw[:, C_HEAD_DIM:]

    chains = [(bb, h) for bb in range(nb) for h in range(C_HEADS)]
    outs = {ch: [] for ch in chains}
    for sub in range(nsub):
        for bb, h in chains:
            vn_ref[bb * C_HEADS + h] = work[bb, sub, h]["u"]
        for cc in range(cps):
            rs = slice(cc * CHUNK, (cc + 1) * CHUNK)
            in_chunk = sub_chunk == cc
            first = {}
            for bb, h in chains:
                ph = work[bb, sub, h]
                s_mat = s_ref[bb, h]
                first[bb, h] = (s_mat, jnp.dot(
                    jnp.concatenate([ph["w"][rs], ph["qg"][rs]], axis=0).astype(BF16),
                    s_mat.astype(BF16), preferred_element_type=F32))
            for bb, h in chains:
                ph = work[bb, sub, h]
                s_mat, ws = first[bb, h]
                vn = vn_ref.at[bb * C_HEADS + h]
                vn[rs, :] = ph["u"][rs] - ws[0:CHUNK]
                vn_all = vn[...].astype(BF16)
                outs[bb, h].append(ws[CHUNK:] + jnp.dot(ph["attn"][rs].astype(BF16), vn_all,
                                                        preferred_element_type=F32))
                kd = jnp.where(in_chunk, ph["kd"], 0.0).astype(BF16)
                s_ref[bb, h] = s_mat * ph["decay"][cc * CHUNK:cc * CHUNK + 1] + lax.dot_general(
                    kd, vn_all, tn, preferred_element_type=F32)

    for bb in range(nb):
        for h in range(C_HEADS):
            hs = slice(h * C_HEAD_DIM, (h + 1) * C_HEAD_DIM)
            o = jnp.concatenate(outs[bb, h], axis=0)
            o_ref[bb, :, hs] = (_rms(o, gn_ref[...]) * _silu(z_ref[bb, :, hs])).astype(BF16)


def _mixer_c(qkv_c, gates, z, conv_w, alog_lane, dtb_lane, gnorm):
    bsz, t, _ = qkv_c.shape
    tm = TM_GDN
    nb = NB_GDN if bsz % NB_GDN == 0 else 1
    row = lambda b, i: (b, i, 0)
    return pl.pallas_call(
        _gdn_kernel,
        grid=(bsz // nb, t // tm),
        in_specs=[
            pl.BlockSpec((nb, tm, C3_W), row),
            pl.BlockSpec((nb, tm, GATE_W), row),
            pl.BlockSpec((nb, tm, C_W), row),
            _const_spec((C_CONV, C3_W)),
            _const_spec((1, GATE_W)),
            _const_spec((1, GATE_W)),
            _const_spec((1, C_HEAD_DIM)),
        ],
        out_specs=pl.BlockSpec((nb, tm, C_W), row),
        out_shape=jax.ShapeDtypeStruct((bsz, t, C_W), BF16),
        scratch_shapes=[
            pltpu.VMEM((nb, tm + 8, C3_W), F32),
            pltpu.VMEM((nb, C_HEADS, C_HEAD_DIM, C_HEAD_DIM), F32),
            pltpu.VMEM((nb * C_HEADS, SB_GDN, C_HEAD_DIM), F32),
        ],
        compiler_params=_params(("parallel", "arbitrary")),
        name="mixer_c_gated_deltanet",
    )(qkv_c, gates, z, conv_w, alog_lane, dtb_lane, gnorm)


def _ffn_kernel(x_ref, oa_ref, ob_ref, oc_ref, mod_ref, wo_ref, g_ref, wg_ref, wu_ref,
                cg_ref, cu_ref, wd_ref, fg_ref, o_ref,
                hb_ref, acc_ref, ug_ref, uu_ref, act_ref, hg_ref, hu_ref, *, final):
    tm = x_ref.shape[1]

    @pl.when(pl.program_id(1) == 0)
    def _():
        hg_ref[...] = jnp.zeros_like(hg_ref)
        hu_ref[...] = jnp.zeros_like(hu_ref)

    gate1 = mod_ref[0, 2:3, :]
    shift2 = mod_ref[0, 3:4, :]
    scale2 = mod_ref[0, 4:5, :]
    gate2 = mod_ref[0, 5:6, :]
    mix = (jnp.dot(oa_ref[0], wo_ref[0:A_W, :], preferred_element_type=F32)
           + jnp.dot(ob_ref[0], wo_ref[A_W:A_W + B_W, :], preferred_element_type=F32)
           + jnp.dot(oc_ref[0], wo_ref[A_W + B_W:, :], preferred_element_type=F32))
    x1 = x_ref[0] + gate1 * mix
    h = _rms(x1, g_ref[...]) * (1.0 + scale2) + shift2
    hb_ref[...] = h.astype(BF16)
    acc_ref[...] = jnp.zeros_like(acc_ref)

    def up_proj(c):
        hb = hb_ref[...]
        ug_ref[c % 2, 8:8 + tm, :] = jnp.dot(hb, wg_ref[c], preferred_element_type=F32)
        uu_ref[c % 2, 8:8 + tm, :] = jnp.dot(hb, wu_ref[c], preferred_element_type=F32)

    def conv(u_ref, hdr_ref, c, w):
        u = u_ref.at[c % 2]
        u[0:8, :] = hdr_ref[c]
        y = w[0:1, :] * u[6:6 + tm, :] + w[1:2, :] * u[7:7 + tm, :] + w[2:3, :] * u[8:8 + tm, :]
        hdr_ref[c] = u[tm:tm + 8, :]
        return y

    up_proj(0)
    for c in range(N_FF_CH + 1):
        if c + 1 < N_FF_CH:
            up_proj(c + 1)
        if c >= 1:
            acc_ref[...] += jnp.dot(act_ref[(c - 1) % 2], wd_ref[c - 1], preferred_element_type=F32)
        if c < N_FF_CH:
            gate = conv(ug_ref, hg_ref, c, cg_ref[c])
            up = conv(uu_ref, hu_ref, c, cu_ref[c])
            act_ref[c % 2] = (_silu(gate) * up).astype(BF16)
    x2 = x1 + gate2 * acc_ref[...]
    if final:
        x2 = _rms(x2, fg_ref[...])
    o_ref[0] = x2


def _out_ffn(x, o_a, o_b, o_c, mod_l, w_out, g, wg, wu, cg, cu, wd, final_g, final):
    bsz, t, _ = x.shape
    tm = TM_FFN
    row = lambda b, i: (b, i, 0)
    return pl.pallas_call(
        functools.partial(_ffn_kernel, final=final),
        grid=(bsz, t // tm),
        in_specs=[
            pl.BlockSpec((1, tm, D_MODEL), row),
            pl.BlockSpec((1, tm, A_W), row),
            pl.BlockSpec((1, tm, B_W), row),
            pl.BlockSpec((1, tm, C_W), row),
            pl.BlockSpec((1, 6, D_MODEL), lambda b, i: (b, 0, 0)),
            _const_spec((D_MODEL, D_MODEL)),
            _const_spec((1, D_MODEL)),
            _const_spec((N_FF_CH, D_MODEL, FF_CH)),
            _const_spec((N_FF_CH, D_MODEL, FF_CH)),
            _const_spec((N_FF_CH, FFN_CONV, FF_CH)),
            _const_spec((N_FF_CH, FFN_CONV, FF_CH)),
            _const_spec((N_FF_CH, FF_CH, D_MODEL)),
            _const_spec((1, D_MODEL)),
        ],
        out_specs=pl.BlockSpec((1, tm, D_MODEL), row),
        out_shape=jax.ShapeDtypeStruct((bsz, t, D_MODEL), F32),
        scratch_shapes=[
            pltpu.VMEM((tm, D_MODEL), BF16),
            pltpu.VMEM((tm, D_MODEL), F32),
            pltpu.VMEM((2, tm + 8, FF_CH), F32),
            pltpu.VMEM((2, tm + 8, FF_CH), F32),
            pltpu.VMEM((2, tm, FF_CH), BF16),
            pltpu.VMEM((N_FF_CH, 8, FF_CH), F32),
            pltpu.VMEM((N_FF_CH, 8, FF_CH), F32),
        ],
        compiler_params=_params(("parallel", "arbitrary")),
        name="outproj_convmlp",
    )(x, o_a, o_b, o_c, mod_l, w_out, g, wg, wu, cg, cu, wd, final_g)


def _chunk_cols(w, n, width):
    return w.reshape(w.shape[0], n, width).transpose(1, 0, 2)


def _lane_vec(values, start):
    return jnp.zeros((1, GATE_W), F32).at[0, start:start + values.shape[0]].set(values)


def kernel(x, c, ada_w, ada_b, norm_mix_g, norm_ffn_g, w_in, w_out, rel_bias, fgate_bias,
           gdn_conv_w, gdn_A_log, gdn_dt_bias, gdn_norm_g, ffn_w_up, ffn_conv_w,
           ffn_w_down, final_norm_g):
    mod = _modulation(c, ada_w, ada_b)
    final_g = final_norm_g.reshape(1, D_MODEL)
    bias_tiles = _rel_bias_tiles(rel_bias)
    for l in range(DEPTH):
        qkv_ab, qkv_c, z, gates = _inproj(
            x, mod[l], norm_mix_g[l].reshape(1, D_MODEL), _arrange_w_in(w_in[l]))
        f_cols = _fgate(gates, _lane_vec(fgate_bias[l], F_LANE))
        o_a = _mixer_a(qkv_ab, bias_tiles[l])
        o_b = _mixer_b(qkv_ab, f_cols)
        o_c = _mixer_c(qkv_c, gates, z, gdn_conv_w[l],
                       _lane_vec(gdn_A_log[l], A_LANE), _lane_vec(gdn_dt_bias[l], A_LANE),
                       gdn_norm_g[l].reshape(1, C_HEAD_DIM))
        w_up = ffn_w_up[l].astype(BF16)
        x = _out_ffn(
            x, o_a, o_b, o_c, mod[l], w_out[l].astype(BF16),
            norm_ffn_g[l].reshape(1, D_MODEL),
            _chunk_cols(w_up[:, :D_FF], N_FF_CH, FF_CH),
            _chunk_cols(w_up[:, D_FF:], N_FF_CH, FF_CH),
            _chunk_cols(ffn_conv_w[l][:, :D_FF], N_FF_CH, FF_CH),
            _chunk_cols(ffn_conv_w[l][:, D_FF:], N_FF_CH, FF_CH),
            ffn_w_down[l].astype(BF16).reshape(N_FF_CH, FF_CH, D_MODEL),
            final_g, final=(l == DEPTH - 1))
    return x
```

```python
import functools

import jax
import jax.numpy as jnp
from jax import lax
from jax.experimental import pallas as pl
from jax.experimental.pallas import tpu as pltpu

F32 = jnp.float32
BF16 = jnp.bfloat16

D_MODEL = 1024
DEPTH = 2
CHUNK = 64
EPS = 1e-6
A_HEADS = 4
A_HEAD_DIM = 64
A_LEFT_CHUNKS = 8
REL_PAST = 256
REL_FUTURE = CHUNK - 1
B_HEADS = 4
B_HEAD_DIM = 64
C_HEADS = 4
C_HEAD_DIM = 128
C_CONV = 4
D_FF = 2816
FFN_CONV = 3
A_W = A_HEADS * A_HEAD_DIM
B_W = B_HEADS * B_HEAD_DIM
C_W = C_HEADS * C_HEAD_DIM
AB_W = 3 * A_W + 3 * B_W
C3_W = 3 * C_W
LANES = 128
GATE_W = LANES
F_LANE, BETA_LANE, A_LANE = 0, 4, 8
NEG = -1e30
LOG2E = 1.4426950408889634
F_PARTS = 3
VMEM_LIMIT = 56 * 1024 * 1024

TM_PROJ = 512
TF_GATE = 1024
TQ_A = 512
TQ_B = 512
TM_GDN = 256
SB_GDN = 128
NB_GDN = 2
TM_FFN = 512
FF_CH = 256
N_FF_CH = D_FF // FF_CH


def _sigmoid(x):
    return 1.0 / (1.0 + jnp.exp(-x))


def _silu(x):
    return x * _sigmoid(x)


def _softplus(x):
    return jnp.maximum(x, 0.0) + jnp.log1p(jnp.exp(-jnp.abs(x)))


def _log_sigmoid(x):
    return jnp.minimum(x, 0.0) - jnp.log1p(jnp.exp(-jnp.abs(x)))


def _rms(x, g):
    return x * lax.rsqrt(jnp.mean(x * x, axis=-1, keepdims=True) + EPS) * g


def _params(sem):
    return pltpu.CompilerParams(dimension_semantics=sem, vmem_limit_bytes=VMEM_LIMIT)


def _const_spec(shape):
    nd = len(shape)
    return pl.BlockSpec(shape, lambda *_: (0,) * nd, pipeline_mode=pl.Buffered(1))


def _mod_kernel(c_ref, w_ref, b_ref, o_ref):
    c_act = _silu(c_ref[...])
    o_ref[0] = jnp.dot(c_act, w_ref[0], preferred_element_type=F32) + b_ref[0]


def _modulation(c, ada_w, ada_b):
    bsz = c.shape[0]
    rows = 8
    cp = jnp.zeros((rows, D_MODEL), F32).at[:bsz].set(c)
    ncol = 6 * D_MODEL
    cb = 1536
    out = pl.pallas_call(
        _mod_kernel,
        grid=(DEPTH, ncol // cb),
        in_specs=[
            pl.BlockSpec((rows, D_MODEL), lambda l, j: (0, 0)),
            pl.BlockSpec((1, D_MODEL, cb), lambda l, j: (l, 0, j)),
            pl.BlockSpec((1, 1, cb), lambda l, j: (l, 0, j)),
        ],
        out_specs=pl.BlockSpec((1, rows, cb), lambda l, j: (l, 0, j)),
        out_shape=jax.ShapeDtypeStruct((DEPTH, rows, ncol), F32),
        compiler_params=_params(("arbitrary", "arbitrary")),
        name="adaln_modulation",
    )(cp, ada_w, ada_b.reshape(DEPTH, 1, ncol))
    return out[:, :bsz].reshape(DEPTH, bsz, 6, D_MODEL)


_PROJ_W = AB_W + C3_W + C_W + GATE_W


def _inproj_kernel(x_ref, mod_ref, g_ref, w_ref, oab_ref, oc_ref, oz_ref, og_ref):
    x = x_ref[0]
    shift = mod_ref[0, 0:1, :]
    scale = mod_ref[0, 1:2, :]
    h = _rms(x, g_ref[...]) * (1.0 + scale) + shift
    hb = h.astype(BF16)
    cw = 512
    for j in range(AB_W // cw):
        oab_ref[0, :, j * cw:(j + 1) * cw] = jnp.dot(
            hb, w_ref[:, j * cw:(j + 1) * cw], preferred_element_type=F32).astype(BF16)
    for j in range(C3_W // cw):
        oc_ref[0, :, j * cw:(j + 1) * cw] = jnp.dot(
            hb, w_ref[:, AB_W + j * cw:AB_W + (j + 1) * cw], preferred_element_type=F32)
    oz_ref[0] = jnp.dot(hb, w_ref[:, AB_W + C3_W:AB_W + C3_W + C_W],
                        preferred_element_type=F32)
    og_ref[0] = jnp.dot(hb, w_ref[:, AB_W + C3_W + C_W:], preferred_element_type=F32)


def _inproj(x, mod_l, g, w):
    bsz, t, _ = x.shape
    tm = TM_PROJ
    row = lambda b, i: (b, i, 0)
    return pl.pallas_call(
        _inproj_kernel,
        grid=(bsz, t // tm),
        in_specs=[
            pl.BlockSpec((1, tm, D_MODEL), row),
            pl.BlockSpec((1, 6, D_MODEL), lambda b, i: (b, 0, 0)),
            _const_spec((1, D_MODEL)),
            _const_spec((D_MODEL, _PROJ_W)),
        ],
        out_specs=[
            pl.BlockSpec((1, tm, AB_W), row),
            pl.BlockSpec((1, tm, C3_W), row),
            pl.BlockSpec((1, tm, C_W), row),
            pl.BlockSpec((1, tm, GATE_W), row),
        ],
        out_shape=[
            jax.ShapeDtypeStruct((bsz, t, AB_W), BF16),
            jax.ShapeDtypeStruct((bsz, t, C3_W), F32),
            jax.ShapeDtypeStruct((bsz, t, C_W), F32),
            jax.ShapeDtypeStruct((bsz, t, GATE_W), F32),
        ],
        compiler_params=_params(("parallel", "parallel")),
        name="inproj",
    )(x, mod_l, g, w)


def _arrange_w_in(w_in_l):
    a_end = 3 * A_W
    b_end = a_end + 3 * B_W
    f_end = b_end + B_HEADS
    c_end = f_end + C3_W
    beta_end = c_end + C_HEADS
    a_gate_end = beta_end + C_HEADS
    qa_scale = A_HEAD_DIM ** -0.5
    qb_scale = B_HEAD_DIM ** -0.5 * LOG2E
    w_ab = jnp.concatenate([
        w_in_l[:, 0:A_W] * qa_scale, w_in_l[:, A_W:a_end],
        w_in_l[:, a_end:a_end + B_W] * qb_scale, w_in_l[:, a_end + B_W:b_end]], axis=1)
    w_gate = jnp.concatenate([
        w_in_l[:, b_end:f_end], w_in_l[:, c_end:beta_end], w_in_l[:, beta_end:a_gate_end],
        jnp.zeros((D_MODEL, GATE_W - B_HEADS - 2 * C_HEADS), F32)], axis=1)
    w = jnp.concatenate([w_ab, w_in_l[:, f_end:c_end], w_in_l[:, a_gate_end:], w_gate], axis=1)
    return w.astype(BF16)


def _fgate_kernel(g_ref, fb_ref, o_ref, carry_ref):
    @pl.when(pl.program_id(1) == 0)
    def _():
        carry_ref[...] = jnp.zeros_like(carry_ref)

    tf = g_ref.shape[1]
    lane = lax.broadcasted_iota(jnp.int32, (tf, GATE_W), 1)
    row = lax.broadcasted_iota(jnp.int32, (tf, GATE_W), 0)
    cs = jnp.where(lane < B_HEADS, _log_sigmoid(g_ref[0] + fb_ref[...]), 0.0)
    d = 1
    while d < tf:
        cs = cs + jnp.where(row >= d, pltpu.roll(cs, d, axis=0), 0.0)
        d *= 2
    cs = cs + carry_ref[0:1, :]
    carry_ref[...] = jnp.broadcast_to(cs[tf - 1:tf, :], carry_ref.shape)
    rest = cs * (-LOG2E)
    out = jnp.zeros_like(rest)
    for i in range(F_PARTS):
        part = rest.astype(BF16).astype(F32)
        rest = rest - part
        out = out + (part if i == 0 else pltpu.roll(part, i * B_HEADS, axis=1))
    o_ref[0] = out.astype(BF16)


def _fgate(gates, fbias_lane):
    bsz, t, _ = gates.shape
    tf = min(TF_GATE, t)
    return pl.pallas_call(
        _fgate_kernel,
        grid=(bsz, t // tf),
        in_specs=[
            pl.BlockSpec((1, tf, GATE_W), lambda b, i: (b, i, 0)),
            _const_spec((1, GATE_W)),
        ],
        out_specs=pl.BlockSpec((1, tf, GATE_W), lambda b, i: (b, i, 0)),
        out_shape=jax.ShapeDtypeStruct((bsz, t, GATE_W), BF16),
        scratch_shapes=[pltpu.VMEM((8, LANES), F32)],
        compiler_params=_params(("parallel", "arbitrary")),
        name="fgate_cumsum",
    )(gates, fbias_lane)


def _mixa_kernel(q_ref, kp_ref, kc_ref, vp_ref, vc_ref, bias_ref, o_ref):
    i = pl.program_id(1)
    tq = q_ref.shape[1]
    lane = lax.broadcasted_iota(jnp.int32, (1, LANES), 1)
    lo = lane < A_HEAD_DIM
    has_prev = i > 0
    nt = (((1,), (1,)), ((), ()))
    hq = tq // 2
    spans = ((slice(0, hq), slice(0, tq), slice(0, hq)),
             (slice(hq, tq), slice(hq, tq), slice(0, tq)))
    units = [(h, half) for h in range(A_HEADS) for half in range(2)]
    logits = {}
    for h, half in units:
        sl = slice((h // 2) * LANES, (h // 2 + 1) * LANES)
        rows, pk, ck = spans[half]
        q = q_ref[0, rows, sl]
        qh = jnp.where(lo if h % 2 == 0 else ~lo, q, jnp.zeros_like(q))
        logits[h, half] = (lax.dot_general(qh, kp_ref[0, pk, sl], nt, preferred_element_type=F32),
                           lax.dot_general(qh, kc_ref[0, ck, sl], nt, preferred_element_type=F32))
    outs = {}
    for h, half in units:
        sl = slice((h // 2) * LANES, (h // 2 + 1) * LANES)
        rows, pk, ck = spans[half]
        s_p = jnp.where(has_prev, logits[h, half][0] + bias_ref[h, rows, pk], NEG)
        s_c = logits[h, half][1] + bias_ref[h, rows, tq + ck.start:tq + ck.stop]
        m = jnp.maximum(jnp.max(s_p, axis=-1, keepdims=True), jnp.max(s_c, axis=-1, keepdims=True))
        p_p = jnp.exp(s_p - m)
        p_c = jnp.exp(s_c - m)
        l = jnp.sum(p_p, axis=-1, keepdims=True) + jnp.sum(p_c, axis=-1, keepdims=True)
        o = (jnp.dot(p_p.astype(BF16), vp_ref[0, pk, sl], preferred_element_type=F32)
             + jnp.dot(p_c.astype(BF16), vc_ref[0, ck, sl], preferred_element_type=F32))
        outs[h, half] = o / l
    for pair in range(A_HEADS // 2):
        for half in range(2):
            o_ref[0, spans[half][0], pair * LANES:(pair + 1) * LANES] = jnp.where(
                lo, outs[2 * pair, half], outs[2 * pair + 1, half]).astype(BF16)


def _mixer_a(qkv_ab, bias):
    bsz, t, _ = qkv_ab.shape
    tq = TQ_A
    prev = lambda col: (lambda b, i: (b, jnp.maximum(i - 1, 0), col))
    cur = lambda col: (lambda b, i: (b, i, col))
    blk = (1, tq, A_W)
    return pl.pallas_call(
        _mixa_kernel,
        grid=(bsz, t // tq),
        in_specs=[
            pl.BlockSpec(blk, cur(0)),
            pl.BlockSpec(blk, prev(1)), pl.BlockSpec(blk, cur(1)),
            pl.BlockSpec(blk, prev(2)), pl.BlockSpec(blk, cur(2)),
            _const_spec((A_HEADS, tq, 2 * tq)),
        ],
        out_specs=pl.BlockSpec(blk, cur(0)),
        out_shape=jax.ShapeDtypeStruct((bsz, t, A_W), BF16),
        compiler_params=_params(("parallel", "parallel")),
        name="mixer_a_chunk_attn",
    )(qkv_ab, qkv_ab, qkv_ab, qkv_ab, qkv_ab, bias)


_REL_PAD = 384
_REL_ROLL_W = 4 * TQ_A


def _relbias_kernel(tab_ref, o_ref, row_ref):
    tq = o_ref.shape[1]
    m = lax.broadcasted_iota(jnp.int32, (_REL_PAD, _REL_ROLL_W), 1)
    m = jnp.where(m < _REL_ROLL_W // 2, m, m - _REL_ROLL_W)
    idx = jnp.clip(tq - m, -REL_FUTURE, REL_PAST) + REL_FUTURE
    n = lax.broadcasted_iota(jnp.int32, (_REL_PAD, _REL_ROLL_W), 0)
    onehot = (n == idx).astype(F32)
    row_ref[...] = jnp.dot(tab_ref[...], onehot, precision=lax.Precision.HIGHEST,
                           preferred_element_type=F32)
    row = row_ref[pl.ds(pl.program_id(0), 1), :]
    rolled = pltpu.roll(jnp.broadcast_to(row, (tq, _REL_ROLL_W)), 0, 1, stride=1, stride_axis=0)
    qc = lax.broadcasted_iota(jnp.int32, (tq, 2 * tq), 0) // CHUNK
    kc = lax.broadcasted_iota(jnp.int32, (tq, 2 * tq), 1) // CHUNK - tq // CHUNK
    valid = (kc <= qc) & (kc >= qc - A_LEFT_CHUNKS)
    o_ref[0] = jnp.where(valid, rolled[:, 0:2 * tq], NEG)


def _rel_bias_tiles(rel_bias):
    nrel = rel_bias.shape[-1]
    nrow = DEPTH * A_HEADS
    tab = jnp.zeros((nrow, _REL_PAD), F32).at[:, :nrel].set(rel_bias.reshape(nrow, nrel))
    out = pl.pallas_call(
        _relbias_kernel,
        grid=(nrow,),
        in_specs=[pl.BlockSpec((nrow, _REL_PAD), lambda h: (0, 0))],
        out_specs=pl.BlockSpec((1, TQ_A, 2 * TQ_A), lambda h: (h, 0, 0)),
        out_shape=jax.ShapeDtypeStruct((nrow, TQ_A, 2 * TQ_A), F32),
        scratch_shapes=[pltpu.VMEM((nrow, _REL_ROLL_W), F32)],
        compiler_params=_params(("arbitrary",)),
        name="relbias_toeplitz",
    )(tab)
    return out.reshape(DEPTH, A_HEADS, TQ_A, 2 * TQ_A)


def _mixb_kernel(q_ref, k_ref, v_ref, f_ref, o_ref, s_ref, p_ref, m_ref, l_ref, acc_ref):
    pair = pl.program_id(1)
    i = pl.program_id(2)
    tq = q_ref.shape[1]
    tk = tq
    lane = lax.broadcasted_iota(jnp.int32, (1, LANES), 1)
    lo = lane < B_HEAD_DIM
    q = q_ref[0]
    zero = jnp.zeros_like(q)
    qs = []
    for hh in range(2):
        head = 2 * pair + hh
        picks = functools.reduce(jnp.logical_or, [lane == head + t * B_HEADS for t in range(F_PARTS)])
        ones = jnp.broadcast_to(jnp.where(picks, 1.0, 0.0).astype(F32), (tq, LANES)).astype(BF16)
        qs.append(jnp.concatenate([jnp.where(lo if hh == 0 else ~lo, q, zero), ones], axis=1))
    nt = (((1,), (1,)), ((), ()))
    m_ref[...] = jnp.full(m_ref.shape, NEG, F32)
    l_ref[...] = jnp.zeros(l_ref.shape, F32)
    acc_ref[...] = jnp.zeros(acc_ref.shape, F32)
    p_ref[...] = jnp.zeros(p_ref.shape, BF16)

    def logits(j):
        off = pl.multiple_of(j * tk, tk)
        kf = jnp.concatenate([k_ref[0, pl.ds(off, tk), :], f_ref[0, pl.ds(off, tk), :]], axis=1)
        return [lax.dot_general(qs[hh], kf, nt, preferred_element_type=F32) for hh in range(2)]

    def pending_pv(j):
        off = pl.multiple_of(jnp.maximum(j - 1, 0) * tk, tk)
        v = v_ref[0, pl.ds(off, tk), :]
        return [jnp.dot(p_ref[hh], v, preferred_element_type=F32) for hh in range(2)]

    def softmax(hh, diagonal):
        s = s_ref[hh]
        if diagonal:
            r = lax.broadcasted_iota(jnp.int32, (tq, tk), 0)
            c = lax.broadcasted_iota(jnp.int32, (tq, tk), 1)
            s = jnp.where(c <= r, s, NEG)
        tiles = [s[:, t * LANES:(t + 1) * LANES] for t in range(tk // LANES)]
        m_tile = functools.reduce(jnp.maximum, tiles)
        m_prev = m_ref[hh]
        m_new = jnp.maximum(m_prev, jnp.max(m_tile, axis=-1, keepdims=True))
        alpha = jnp.exp2(m_prev - m_new)
        p_tiles = [jnp.exp2(t - m_new) for t in tiles]
        l_ref[hh] = alpha * l_ref[hh] + functools.reduce(jnp.add, p_tiles)
        m_ref[hh] = m_new
        return alpha, jnp.concatenate(p_tiles, axis=1).astype(BF16)

    def store(s_pair):
        for hh in range(2):
            s_ref[hh] = s_pair[hh]

    store(logits(0))

    def body(j, carry):
        pv = pending_pv(j)
        nxt = logits(j + 1)
        for hh in range(2):
            alpha, p = softmax(hh, False)
            p_ref[hh] = p
            acc_ref[hh] = alpha * (acc_ref[hh] + pv[hh])
        store(nxt)
        return carry

    lax.fori_loop(0, i, body, 0)
    pv = pending_pv(i)
    v = v_ref[0, pl.ds(pl.multiple_of(i * tk, tk), tk), :]
    outs = []
    for hh in range(2):
        alpha, p = softmax(hh, True)
        acc = alpha * (acc_ref[hh] + pv[hh]) + jnp.dot(p, v, preferred_element_type=F32)
        outs.append(acc / jnp.sum(l_ref[hh], axis=-1, keepdims=True))
    o_ref[0] = jnp.where(lo, outs[0], outs[1]).astype(BF16)


def _mixer_b(qkv_ab, f_cols):
    bsz, t, _ = qkv_ab.shape
    tq = TQ_B
    base = 3 * A_W // LANES
    npair = B_HEADS // 2
    return pl.pallas_call(
        _mixb_kernel,
        grid=(bsz, npair, t // tq),
        in_specs=[
            pl.BlockSpec((1, tq, LANES), lambda b, p, i: (b, i, base + p)),
            pl.BlockSpec((1, t, LANES), lambda b, p, i: (b, 0, base + npair + p)),
            pl.BlockSpec((1, t, LANES), lambda b, p, i: (b, 0, base + 2 * npair + p)),
            pl.BlockSpec((1, t, GATE_W), lambda b, p, i: (b, 0, 0)),
        ],
        out_specs=pl.BlockSpec((1, tq, LANES), lambda b, p, i: (b, i, p)),
        out_shape=jax.ShapeDtypeStruct((bsz, t, B_W), BF16),
        scratch_shapes=[
            pltpu.VMEM((2, tq, tq), F32),
            pltpu.VMEM((2, tq, tq), BF16),
            pltpu.VMEM((2, tq, LANES), F32),
            pltpu.VMEM((2, tq, LANES), F32),
            pltpu.VMEM((2, tq, LANES), F32),
        ],
        compiler_params=_params(("parallel", "parallel", "parallel")),
        name="mixer_b_forgetting_attn",
    )(qkv_ab, qkv_ab, qkv_ab, f_cols)


def _gdn_kernel(x_ref, gate_ref, z_ref, cw_ref, alog_ref, dtb_ref, gn_ref, o_ref,
                xs_ref, s_ref, vn_ref):
    nb, tm = x_ref.shape[0], x_ref.shape[1]
    sb = SB_GDN
    nsub = tm // sb
    cps = sb // CHUNK

    @pl.when(pl.program_id(1) == 0)
    def _():
        xs_ref[:, 0:8, :] = jnp.zeros((nb, 8, C3_W), F32)
        s_ref[...] = jnp.zeros_like(s_ref)

    ri = lax.broadcasted_iota(jnp.int32, (sb, sb), 0)
    ci = lax.broadcasted_iota(jnp.int32, (sb, sb), 1)
    same = (ri // CHUNK) == (ci // CHUNK)
    tril = same & (ci <= ri)
    strict = same & (ci < ri)
    eye = (ri == ci).astype(F32)

    def level_mask(b):
        return ((ri // (2 * b)) == (ci // (2 * b))) & (((ri // b) % 2) == 1) & (((ci // b) % 2) == 0)

    pos = lax.broadcasted_iota(jnp.int32, (tm, GATE_W), 0) % CHUNK
    sub_chunk = lax.broadcasted_iota(jnp.int32, (sb, 1), 0) // CHUNK
    nt = (((1,), (1,)), ((), ()))
    tn = (((0,), (0,)), ((), ()))

    units = [(bb, sub, h) for bb in range(nb) for sub in range(nsub) for h in range(C_HEADS)]
    gates = {}
    qkvs = {}
    for bb in range(nb):
        xs = xs_ref.at[bb]
        xs[8:8 + tm, :] = x_ref[bb]
        y = cw_ref[0:1, :] * xs[5:5 + tm, :]
        for tap in range(1, C_CONV):
            y = y + cw_ref[tap:tap + 1, :] * xs[5 + tap:5 + tap + tm, :]
        xs[0:8, :] = xs[tm:tm + 8, :]
        qkvs[bb] = _silu(y)

        gt = gate_ref[bb]
        beta = _sigmoid(gt)
        g = -jnp.exp(alog_ref[...]) * _softplus(gt + dtb_ref[...])
        gc = g
        d = 1
        while d < CHUNK:
            gc = gc + jnp.where(pos >= d, pltpu.roll(gc, d, axis=0), 0.0)
            d *= 2
        gl = jnp.concatenate(
            [jnp.broadcast_to(gc[(c + 1) * CHUNK - 1:(c + 1) * CHUNK, :], (CHUNK, GATE_W))
             for c in range(tm // CHUNK)], axis=0)
        gates[bb] = (beta, gc, gl, gc.T)

    work = {}
    for u in units:
        bb, sub, h = u
        rows = slice(sub * sb, (sub + 1) * sb)
        qkv = qkvs[bb]
        beta, gc, gl, gc_t = gates[bb]
        qh = qkv[rows, h * C_HEAD_DIM:(h + 1) * C_HEAD_DIM]
        kh = qkv[rows, C_W + h * C_HEAD_DIM:C_W + (h + 1) * C_HEAD_DIM]
        vh = qkv[rows, 2 * C_W + h * C_HEAD_DIM:2 * C_W + (h + 1) * C_HEAD_DIM]
        qh = qh * lax.rsqrt(jnp.sum(qh * qh, axis=-1, keepdims=True) + EPS) * (C_HEAD_DIM ** -0.5)
        kh = kh * lax.rsqrt(jnp.sum(kh * kh, axis=-1, keepdims=True) + EPS)
        b_col = beta[rows, BETA_LANE + h:BETA_LANE + h + 1]
        gc_col = gc[rows, A_LANE + h:A_LANE + h + 1]
        gl_col = gl[rows, A_LANE + h:A_LANE + h + 1]
        gc_row = gc_t[A_LANE + h:A_LANE + h + 1, rows]
        kbeta = kh * b_col
        lm = jnp.where(tril, jnp.exp(jnp.where(tril, gc_col - gc_row, 0.0)), 0.0)
        kq = lax.dot_general(jnp.concatenate([kbeta, qh], axis=0).astype(BF16), kh.astype(BF16),
                             nt, preferred_element_type=F32)
        amat = jnp.where(strict, kq[0:sb] * lm, 0.0)
        work[u] = dict(
            amat=amat, attn=jnp.where(tril, kq[sb:2 * sb] * lm, 0.0),
            rhs=jnp.concatenate([vh * b_col, kbeta * jnp.exp(gc_col)], axis=1).astype(BF16),
            qg=qh * jnp.exp(gc_col), kd=kh * jnp.exp(gl_col - gc_col), decay=jnp.exp(gl_col),
            tinv=eye - jnp.where(level_mask(1), amat, 0.0))

    b = 2
    while b < CHUNK:
        mask = level_mask(b)
        tas = {}
        for u in units:
            t16 = work[u]["tinv"].astype(BF16)
            work[u]["t16"] = t16
            tas[u] = jnp.dot(t16, jnp.where(mask, work[u]["amat"], 0.0).astype(BF16),
                             preferred_element_type=F32)
        for u in units:
            work[u]["tinv"] = work[u]["tinv"] - jnp.dot(tas[u].astype(BF16), work[u]["t16"],
                                                        preferred_element_type=F32)
        b *= 2
    for u in units:
        uw = jnp.dot(work[u]["tinv"].astype(BF16), work[u]["rhs"], preferred_element_type=F32)
        work[u]["u"] = uw[:, :C_HEAD_DIM]
        work[u]["w"] = uw[:, C_HEAD_DIM:]

    chains = [(bb, h) for bb in range(nb) for h in range(C_HEADS)]
    outs = {ch: [] for ch in chains}
    for sub in range(nsub):
        for bb, h in chains:
            vn_ref[bb * C_HEADS + h] = work[bb, sub, h]["u"]
        for cc in range(cps):
            rs = slice(cc * CHUNK, (cc + 1) * CHUNK)
            in_chunk = sub_chunk == cc
            first = {}
            for bb, h in chains:
                ph = work[bb, sub, h]
                s_mat = s_ref[bb, h]
                first[bb, h] = (s_mat, jnp.dot(
                    jnp.concatenate([ph["w"][rs], ph["qg"][rs]], axis=0).astype(BF16),
                    s_mat.astype(BF16), preferred_element_type=F32))
            for bb, h in chains:
                ph = work[bb, sub, h]
                s_mat, ws = first[bb, h]
                vn = vn_ref.at[bb * C_HEADS + h]
                vn[rs, :] = ph["u"][rs] - ws[0:CHUNK]
                vn_all = vn[...].astype(BF16)
                outs[bb, h].append(ws[CHUNK:] + jnp.dot(ph["attn"][rs].astype(BF16), vn_all,
                                                        preferred_element_type=F32))
                kd = jnp.where(in_chunk, ph["kd"], 0.0).astype(BF16)
                s_ref[bb, h] = s_mat * ph["decay"][cc * CHUNK:cc * CHUNK + 1] + lax.dot_general(
                    kd, vn_all, tn, preferred_element_type=F32)

    for bb in range(nb):
        for h in range(C_HEADS):
            hs = slice(h * C_HEAD_DIM, (h + 1) * C_HEAD_DIM)
            o = jnp.concatenate(outs[bb, h], axis=0)
            o_ref[bb, :, hs] = (_rms(o, gn_ref[...]) * _silu(z_ref[bb, :, hs])).astype(BF16)


def _mixer_c(qkv_c, gates, z, conv_w, alog_lane, dtb_lane, gnorm):
    bsz, t, _ = qkv_c.shape
    tm = TM_GDN
    nb = NB_GDN if bsz % NB_GDN == 0 else 1
    row = lambda b, i: (b, i, 0)
    return pl.pallas_call(
        _gdn_kernel,
        grid=(bsz // nb, t // tm),
        in_specs=[
            pl.BlockSpec((nb, tm, C3_W), row),
            pl.BlockSpec((nb, tm, GATE_W), row),
            pl.BlockSpec((nb, tm, C_W), row),
            _const_spec((C_CONV, C3_W)),
            _const_spec((1, GATE_W)),
            _const_spec((1, GATE_W)),
            _const_spec((1, C_HEAD_DIM)),
        ],
        out_specs=pl.BlockSpec((nb, tm, C_W), row),
        out_shape=jax.ShapeDtypeStruct((bsz, t, C_W), BF16),
        scratch_shapes=[
            pltpu.VMEM((nb, tm + 8, C3_W), F32),
            pltpu.VMEM((nb, C_HEADS, C_HEAD_DIM, C_HEAD_DIM), F32),
            pltpu.VMEM((nb * C_HEADS, SB_GDN, C_HEAD_DIM), F32),
        ],
        compiler_params=_params(("parallel", "arbitrary")),
        name="mixer_c_gated_deltanet",
    )(qkv_c, gates, z, conv_w, alog_lane, dtb_lane, gnorm)


def _ffn_kernel(x_ref, oa_ref, ob_ref, oc_ref, mod_ref, wo_ref, g_ref, wup_ref, cw_ref, wd_ref,
                fg_ref, o_ref,
                hb_ref, acc_ref, ug_ref, uu_ref, act_ref, hg_ref, hu_ref, *, final):
    tm = x_ref.shape[1]

    @pl.when(pl.program_id(1) == 0)
    def _():
        hg_ref[...] = jnp.zeros_like(hg_ref)
        hu_ref[...] = jnp.zeros_like(hu_ref)

    gate1 = mod_ref[0, 2:3, :]
    shift2 = mod_ref[0, 3:4, :]
    scale2 = mod_ref[0, 4:5, :]
    gate2 = mod_ref[0, 5:6, :]
    mix = (jnp.dot(oa_ref[0], wo_ref[0:A_W, :], preferred_element_type=F32)
           + jnp.dot(ob_ref[0], wo_ref[A_W:A_W + B_W, :], preferred_element_type=F32)
           + jnp.dot(oc_ref[0], wo_ref[A_W + B_W:, :], preferred_element_type=F32))
    x1 = x_ref[0] + gate1 * mix
    h = _rms(x1, g_ref[...]) * (1.0 + scale2) + shift2
    hb_ref[...] = h.astype(BF16)
    acc_ref[...] = jnp.zeros_like(acc_ref)

    def gate_cols(c):
        return slice(c * FF_CH, (c + 1) * FF_CH)

    def up_cols(c):
        return slice(D_FF + c * FF_CH, D_FF + (c + 1) * FF_CH)

    def up_proj(c):
        hb = hb_ref[...]
        ug_ref[c % 2, 8:8 + tm, :] = jnp.dot(hb, wup_ref[:, gate_cols(c)], preferred_element_type=F32)
        uu_ref[c % 2, 8:8 + tm, :] = jnp.dot(hb, wup_ref[:, up_cols(c)], preferred_element_type=F32)

    def conv(u_ref, hdr_ref, c, w):
        u = u_ref.at[c % 2]
        u[0:8, :] = hdr_ref[c]
        y = w[0:1, :] * u[6:6 + tm, :] + w[1:2, :] * u[7:7 + tm, :] + w[2:3, :] * u[8:8 + tm, :]
        hdr_ref[c] = u[tm:tm + 8, :]
        return y

    up_proj(0)
    for c in range(N_FF_CH + 1):
        if c + 1 < N_FF_CH:
            up_proj(c + 1)
        if c >= 1:
            acc_ref[...] += jnp.dot(act_ref[(c - 1) % 2], wd_ref[gate_cols(c - 1), :],
                                    preferred_element_type=F32)
        if c < N_FF_CH:
            gate = conv(ug_ref, hg_ref, c, cw_ref[:, gate_cols(c)])
            up = conv(uu_ref, hu_ref, c, cw_ref[:, up_cols(c)])
            act_ref[c % 2] = (_silu(gate) * up).astype(BF16)
    x2 = x1 + gate2 * acc_ref[...]
    if final:
        x2 = _rms(x2, fg_ref[...])
    o_ref[0] = x2


def _out_ffn(x, o_a, o_b, o_c, mod_l, w_out, g, w_up, conv_w, w_down, final_g, final):
    bsz, t, _ = x.shape
    tm = TM_FFN
    row = lambda b, i: (b, i, 0)
    return pl.pallas_call(
        functools.partial(_ffn_kernel, final=final),
        grid=(bsz, t // tm),
        in_specs=[
            pl.BlockSpec((1, tm, D_MODEL), row),
            pl.BlockSpec((1, tm, A_W), row),
            pl.BlockSpec((1, tm, B_W), row),
            pl.BlockSpec((1, tm, C_W), row),
            pl.BlockSpec((1, 6, D_MODEL), lambda b, i: (b, 0, 0)),
            _const_spec((D_MODEL, D_MODEL)),
            _const_spec((1, D_MODEL)),
            _const_spec((D_MODEL, 2 * D_FF)),
            _const_spec((FFN_CONV, 2 * D_FF)),
            _const_spec((D_FF, D_MODEL)),
            _const_spec((1, D_MODEL)),
        ],
        out_specs=pl.BlockSpec((1, tm, D_MODEL), row),
        out_shape=jax.ShapeDtypeStruct((bsz, t, D_MODEL), F32),
        scratch_shapes=[
            pltpu.VMEM((tm, D_MODEL), BF16),
            pltpu.VMEM((tm, D_MODEL), F32),
            pltpu.VMEM((2, tm + 8, FF_CH), F32),
            pltpu.VMEM((2, tm + 8, FF_CH), F32),
            pltpu.VMEM((2, tm, FF_CH), BF16),
            pltpu.VMEM((N_FF_CH, 8, FF_CH), F32),
            pltpu.VMEM((N_FF_CH, 8, FF_CH), F32),
        ],
        compiler_params=_params(("parallel", "arbitrary")),
        name="outproj_convmlp",
    )(x, o_a, o_b, o_c, mod_l, w_out, g, w_up, conv_w, w_down, final_g)


def _lane_vec(values, start):
    return jnp.zeros((1, GATE_W), F32).at[0, start:start + values.shape[0]].set(values)


def kernel(x, c, ada_w, ada_b, norm_mix_g, norm_ffn_g, w_in, w_out, rel_bias, fgate_bias,
           gdn_conv_w, gdn_A_log, gdn_dt_bias, gdn_norm_g, ffn_w_up, ffn_conv_w,
           ffn_w_down, final_norm_g):
    mod = _modulation(c, ada_w, ada_b)
    final_g = final_norm_g.reshape(1, D_MODEL)
    bias_tiles = _rel_bias_tiles(rel_bias)
    for l in range(DEPTH):
        qkv_ab, qkv_c, z, gates = _inproj(
            x, mod[l], norm_mix_g[l].reshape(1, D_MODEL), _arrange_w_in(w_in[l]))
        f_cols = _fgate(gates, _lane_vec(fgate_bias[l], F_LANE))
        o_a = _mixer_a(qkv_ab, bias_tiles[l])
        o_b = _mixer_b(qkv_ab, f_cols)
        o_c = _mixer_c(qkv_c, gates, z, gdn_conv_w[l],
                       _lane_vec(gdn_A_log[l], A_LANE), _lane_vec(gdn_dt_bias[l], A_LANE),
                       gdn_norm_g[l].reshape(1, C_HEAD_DIM))
        x = _out_ffn(
            x, o_a, o_b, o_c, mod[l], w_out[l].astype(BF16),
            norm_ffn_g[l].reshape(1, D_MODEL),
            ffn_w_up[l].astype(BF16), ffn_conv_w[l], ffn_w_down[l].astype(BF16),
            final_g, final=(l == DEPTH - 1))
    return x
```

```python
import functools

import jax
import jax.numpy as jnp
from jax import lax
from jax.experimental import pallas as pl
from jax.experimental.pallas import tpu as pltpu

F32 = jnp.float32
BF16 = jnp.bfloat16

D_MODEL = 1024
DEPTH = 2
CHUNK = 64
EPS = 1e-6
A_HEADS = 4
A_HEAD_DIM = 64
A_LEFT_CHUNKS = 8
REL_PAST = 256
REL_FUTURE = CHUNK - 1
B_HEADS = 4
B_HEAD_DIM = 64
C_HEADS = 4
C_HEAD_DIM = 128
C_CONV = 4
D_FF = 2816
FFN_CONV = 3
A_W = A_HEADS * A_HEAD_DIM
B_W = B_HEADS * B_HEAD_DIM
C_W = C_HEADS * C_HEAD_DIM
AB_W = 3 * A_W + 3 * B_W
C3_W = 3 * C_W
LANES = 128
GATE_W = LANES
F_LANE, BETA_LANE, A_LANE = 0, 4, 8
NEG = -1e30
LOG2E = 1.4426950408889634
F_PARTS = 3
VMEM_LIMIT = 56 * 1024 * 1024

TM_PROJ = 512
TF_GATE = 1024
TQ_A = 512
TQ_B = 512
TM_GDN = 256
SB_GDN = 128
NB_GDN = 2
TM_FFN = 512
FF_CH = 256
N_FF_CH = D_FF // FF_CH


def _sigmoid(x):
    return 1.0 / (1.0 + jnp.exp(-x))


def _silu(x):
    return x * _sigmoid(x)


def _softplus(x):
    return jnp.maximum(x, 0.0) + jnp.log1p(jnp.exp(-jnp.abs(x)))


def _log_sigmoid(x):
    return jnp.minimum(x, 0.0) - jnp.log1p(jnp.exp(-jnp.abs(x)))


def _rms(x, g):
    return x * lax.rsqrt(jnp.mean(x * x, axis=-1, keepdims=True) + EPS) * g


def _params(sem):
    return pltpu.CompilerParams(dimension_semantics=sem, vmem_limit_bytes=VMEM_LIMIT)


def _const_spec(shape):
    nd = len(shape)
    return pl.BlockSpec(shape, lambda *_: (0,) * nd, pipeline_mode=pl.Buffered(1))


def _mod_kernel(c_ref, w_ref, b_ref, o_ref):
    c_act = _silu(c_ref[...])
    o_ref[0] = jnp.dot(c_act, w_ref[0], preferred_element_type=F32) + b_ref[0]


def _modulation(c, ada_w, ada_b):
    bsz = c.shape[0]
    rows = 8
    cp = jnp.zeros((rows, D_MODEL), F32).at[:bsz].set(c)
    ncol = 6 * D_MODEL
    cb = 1536
    out = pl.pallas_call(
        _mod_kernel,
        grid=(DEPTH, ncol // cb),
        in_specs=[
            pl.BlockSpec((rows, D_MODEL), lambda l, j: (0, 0)),
            pl.BlockSpec((1, D_MODEL, cb), lambda l, j: (l, 0, j)),
            pl.BlockSpec((1, 1, cb), lambda l, j: (l, 0, j)),
        ],
        out_specs=pl.BlockSpec((1, rows, cb), lambda l, j: (l, 0, j)),
        out_shape=jax.ShapeDtypeStruct((DEPTH, rows, ncol), F32),
        compiler_params=_params(("arbitrary", "arbitrary")),
        name="adaln_modulation",
    )(cp, ada_w, ada_b.reshape(DEPTH, 1, ncol))
    return out[:, :bsz].reshape(DEPTH, bsz, 6, D_MODEL)


_PROJ_W = AB_W + C3_W + C_W + GATE_W


def _inproj_kernel(x_ref, mod_ref, g_ref, w_ref, oab_ref, oc_ref, oz_ref, og_ref):
    x = x_ref[0]
    shift = mod_ref[0, 0:1, :]
    scale = mod_ref[0, 1:2, :]
    h = _rms(x, g_ref[...]) * (1.0 + scale) + shift
    hb = h.astype(BF16)
    cw = 512
    for j in range(AB_W // cw):
        oab_ref[0, :, j * cw:(j + 1) * cw] = jnp.dot(
            hb, w_ref[:, j * cw:(j + 1) * cw], preferred_element_type=F32).astype(BF16)
    for j in range(C3_W // cw):
        oc_ref[0, :, j * cw:(j + 1) * cw] = jnp.dot(
            hb, w_ref[:, AB_W + j * cw:AB_W + (j + 1) * cw], preferred_element_type=F32)
    oz_ref[0] = jnp.dot(hb, w_ref[:, AB_W + C3_W:AB_W + C3_W + C_W],
                        preferred_element_type=F32)
    og_ref[0] = jnp.dot(hb, w_ref[:, AB_W + C3_W + C_W:], preferred_element_type=F32)


def _inproj(x, mod_l, g, w):
    bsz, t, _ = x.shape
    tm = TM_PROJ
    row = lambda b, i: (b, i, 0)
    return pl.pallas_call(
        _inproj_kernel,
        grid=(bsz, t // tm),
        in_specs=[
            pl.BlockSpec((1, tm, D_MODEL), row),
            pl.BlockSpec((1, 6, D_MODEL), lambda b, i: (b, 0, 0)),
            _const_spec((1, D_MODEL)),
            _const_spec((D_MODEL, _PROJ_W)),
        ],
        out_specs=[
            pl.BlockSpec((1, tm, AB_W), row),
            pl.BlockSpec((1, tm, C3_W), row),
            pl.BlockSpec((1, tm, C_W), row),
            pl.BlockSpec((1, tm, GATE_W), row),
        ],
        out_shape=[
            jax.ShapeDtypeStruct((bsz, t, AB_W), BF16),
            jax.ShapeDtypeStruct((bsz, t, C3_W), F32),
            jax.ShapeDtypeStruct((bsz, t, C_W), F32),
            jax.ShapeDtypeStruct((bsz, t, GATE_W), F32),
        ],
        compiler_params=_params(("parallel", "parallel")),
        name="inproj",
    )(x, mod_l, g, w)


def _arrange_w_in(w_in_l):
    a_end = 3 * A_W
    b_end = a_end + 3 * B_W
    f_end = b_end + B_HEADS
    c_end = f_end + C3_W
    beta_end = c_end + C_HEADS
    a_gate_end = beta_end + C_HEADS
    qa_scale = A_HEAD_DIM ** -0.5
    qb_scale = B_HEAD_DIM ** -0.5 * LOG2E
    w_ab = jnp.concatenate([
        w_in_l[:, 0:A_W] * qa_scale, w_in_l[:, A_W:a_end],
        w_in_l[:, a_end:a_end + B_W] * qb_scale, w_in_l[:, a_end + B_W:b_end]], axis=1)
    w_gate = jnp.concatenate([
        w_in_l[:, b_end:f_end], w_in_l[:, c_end:beta_end], w_in_l[:, beta_end:a_gate_end],
        jnp.zeros((D_MODEL, GATE_W - B_HEADS - 2 * C_HEADS), F32)], axis=1)
    w = jnp.concatenate([w_ab, w_in_l[:, f_end:c_end], w_in_l[:, a_gate_end:], w_gate], axis=1)
    return w.astype(BF16)


def _fgate_kernel(g_ref, fb_ref, o_ref, carry_ref):
    @pl.when(pl.program_id(1) == 0)
    def _():
        carry_ref[...] = jnp.zeros_like(carry_ref)

    tf = g_ref.shape[1]
    lane = lax.broadcasted_iota(jnp.int32, (tf, GATE_W), 1)
    row = lax.broadcasted_iota(jnp.int32, (tf, GATE_W), 0)
    cs = jnp.where(lane < B_HEADS, _log_sigmoid(g_ref[0] + fb_ref[...]), 0.0)
    d = 1
    while d < tf:
        cs = cs + jnp.where(row >= d, pltpu.roll(cs, d, axis=0), 0.0)
        d *= 2
    cs = cs + carry_ref[0:1, :]
    carry_ref[...] = jnp.broadcast_to(cs[tf - 1:tf, :], carry_ref.shape)
    rest = cs * (-LOG2E)
    out = jnp.zeros_like(rest)
    for i in range(F_PARTS):
        part = rest.astype(BF16).astype(F32)
        rest = rest - part
        out = out + (part if i == 0 else pltpu.roll(part, i * B_HEADS, axis=1))
    o_ref[0] = out.astype(BF16)


def _fgate(gates, fbias_lane):
    bsz, t, _ = gates.shape
    tf = min(TF_GATE, t)
    return pl.pallas_call(
        _fgate_kernel,
        grid=(bsz, t // tf),
        in_specs=[
            pl.BlockSpec((1, tf, GATE_W), lambda b, i: (b, i, 0)),
            _const_spec((1, GATE_W)),
        ],
        out_specs=pl.BlockSpec((1, tf, GATE_W), lambda b, i: (b, i, 0)),
        out_shape=jax.ShapeDtypeStruct((bsz, t, GATE_W), BF16),
        scratch_shapes=[pltpu.VMEM((8, LANES), F32)],
        compiler_params=_params(("parallel", "arbitrary")),
        name="fgate_cumsum",
    )(gates, fbias_lane)


def _mixa_kernel(q_ref, kp_ref, kc_ref, vp_ref, vc_ref, bias_ref, o_ref):
    i = pl.program_id(1)
    tq = q_ref.shape[1]
    lane = lax.broadcasted_iota(jnp.int32, (1, LANES), 1)
    lo = lane < A_HEAD_DIM
    has_prev = i > 0
    nt = (((1,), (1,)), ((), ()))
    logits = []
    for h in range(A_HEADS):
        sl = slice((h // 2) * LANES, (h // 2 + 1) * LANES)
        q = q_ref[0, :, sl]
        qh = jnp.where(lo if h % 2 == 0 else ~lo, q, jnp.zeros_like(q))
        logits.append((lax.dot_general(qh, kp_ref[0, :, sl], nt, preferred_element_type=F32),
                       lax.dot_general(qh, kc_ref[0, :, sl], nt, preferred_element_type=F32)))
    outs = []
    for h in range(A_HEADS):
        sl = slice((h // 2) * LANES, (h // 2 + 1) * LANES)
        s_p = jnp.where(has_prev, logits[h][0] + bias_ref[h, :, 0:tq], NEG)
        s_c = logits[h][1] + bias_ref[h, :, tq:2 * tq]
        m = jnp.maximum(jnp.max(s_p, axis=-1, keepdims=True), jnp.max(s_c, axis=-1, keepdims=True))
        p_p = jnp.exp(s_p - m)
        p_c = jnp.exp(s_c - m)
        l = jnp.sum(p_p, axis=-1, keepdims=True) + jnp.sum(p_c, axis=-1, keepdims=True)
        o = (jnp.dot(p_p.astype(BF16), vp_ref[0, :, sl], preferred_element_type=F32)
             + jnp.dot(p_c.astype(BF16), vc_ref[0, :, sl], preferred_element_type=F32))
        outs.append(o / l)
    for pair in range(A_HEADS // 2):
        o_ref[0, :, pair * LANES:(pair + 1) * LANES] = jnp.where(
            lo, outs[2 * pair], outs[2 * pair + 1]).astype(BF16)


def _mixer_a(qkv_ab, bias):
    bsz, t, _ = qkv_ab.shape
    tq = TQ_A
    prev = lambda col: (lambda b, i: (b, jnp.maximum(i - 1, 0), col))
    cur = lambda col: (lambda b, i: (b, i, col))
    blk = (1, tq, A_W)
    return pl.pallas_call(
        _mixa_kernel,
        grid=(bsz, t // tq),
        in_specs=[
            pl.BlockSpec(blk, cur(0)),
            pl.BlockSpec(blk, prev(1)), pl.BlockSpec(blk, cur(1)),
            pl.BlockSpec(blk, prev(2)), pl.BlockSpec(blk, cur(2)),
            _const_spec((A_HEADS, tq, 2 * tq)),
        ],
        out_specs=pl.BlockSpec(blk, cur(0)),
        out_shape=jax.ShapeDtypeStruct((bsz, t, A_W), BF16),
        compiler_params=_params(("parallel", "parallel")),
        name="mixer_a_chunk_attn",
    )(qkv_ab, qkv_ab, qkv_ab, qkv_ab, qkv_ab, bias)


_REL_PAD = 384
_REL_ROLL_W = 4 * TQ_A


def _relbias_kernel(tab_ref, o_ref, row_ref):
    tq = o_ref.shape[1]
    m = lax.broadcasted_iota(jnp.int32, (_REL_PAD, _REL_ROLL_W), 1)
    m = jnp.where(m < _REL_ROLL_W // 2, m, m - _REL_ROLL_W)
    idx = jnp.clip(tq - m, -REL_FUTURE, REL_PAST) + REL_FUTURE
    n = lax.broadcasted_iota(jnp.int32, (_REL_PAD, _REL_ROLL_W), 0)
    onehot = (n == idx).astype(F32)
    row_ref[...] = jnp.dot(tab_ref[...], onehot, precision=lax.Precision.HIGHEST,
                           preferred_element_type=F32)
    row = row_ref[pl.ds(pl.program_id(0), 1), :]
    rolled = pltpu.roll(jnp.broadcast_to(row, (tq, _REL_ROLL_W)), 0, 1, stride=1, stride_axis=0)
    qc = lax.broadcasted_iota(jnp.int32, (tq, 2 * tq), 0) // CHUNK
    kc = lax.broadcasted_iota(jnp.int32, (tq, 2 * tq), 1) // CHUNK - tq // CHUNK
    valid = (kc <= qc) & (kc >= qc - A_LEFT_CHUNKS)
    o_ref[0] = jnp.where(valid, rolled[:, 0:2 * tq], NEG)


def _rel_bias_tiles(rel_bias):
    nrel = rel_bias.shape[-1]
    nrow = DEPTH * A_HEADS
    tab = jnp.zeros((nrow, _REL_PAD), F32).at[:, :nrel].set(rel_bias.reshape(nrow, nrel))
    out = pl.pallas_call(
        _relbias_kernel,
        grid=(nrow,),
        in_specs=[pl.BlockSpec((nrow, _REL_PAD), lambda h: (0, 0))],
        out_specs=pl.BlockSpec((1, TQ_A, 2 * TQ_A), lambda h: (h, 0, 0)),
        out_shape=jax.ShapeDtypeStruct((nrow, TQ_A, 2 * TQ_A), F32),
        scratch_shapes=[pltpu.VMEM((nrow, _REL_ROLL_W), F32)],
        compiler_params=_params(("arbitrary",)),
        name="relbias_toeplitz",
    )(tab)
    return out.reshape(DEPTH, A_HEADS, TQ_A, 2 * TQ_A)


def _mixb_kernel(q_ref, k_ref, v_ref, f_ref, o_ref, s_ref, p_ref, m_ref, l_ref, acc_ref):
    pair = pl.program_id(1)
    i = pl.program_id(2)
    tq = q_ref.shape[1]
    tk = tq
    lane = lax.broadcasted_iota(jnp.int32, (1, LANES), 1)
    lo = lane < B_HEAD_DIM
    q = q_ref[0]
    zero = jnp.zeros_like(q)
    qs = []
    for hh in range(2):
        head = 2 * pair + hh
        picks = functools.reduce(jnp.logical_or, [lane == head + t * B_HEADS for t in range(F_PARTS)])
        ones = jnp.broadcast_to(jnp.where(picks, 1.0, 0.0).astype(F32), (tq, LANES)).astype(BF16)
        qs.append(jnp.concatenate([jnp.where(lo if hh == 0 else ~lo, q, zero), ones], axis=1))
    nt = (((1,), (1,)), ((), ()))
    m_ref[...] = jnp.full(m_ref.shape, NEG, F32)
    l_ref[...] = jnp.zeros(l_ref.shape, F32)
    acc_ref[...] = jnp.zeros(acc_ref.shape, F32)
    p_ref[...] = jnp.zeros(p_ref.shape, BF16)

    def logits(j):
        off = pl.multiple_of(j * tk, tk)
        kf = jnp.concatenate([k_ref[0, pl.ds(off, tk), :], f_ref[0, pl.ds(off, tk), :]], axis=1)
        return [lax.dot_general(qs[hh], kf, nt, preferred_element_type=F32) for hh in range(2)]

    def pending_pv(j):
        off = pl.multiple_of(jnp.maximum(j - 1, 0) * tk, tk)
        v = v_ref[0, pl.ds(off, tk), :]
        return [jnp.dot(p_ref[hh], v, preferred_element_type=F32) for hh in range(2)]

    def softmax(hh, diagonal):
        s = s_ref[hh]
        if diagonal:
            r = lax.broadcasted_iota(jnp.int32, (tq, tk), 0)
            c = lax.broadcasted_iota(jnp.int32, (tq, tk), 1)
            s = jnp.where(c <= r, s, NEG)
        tiles = [s[:, t * LANES:(t + 1) * LANES] for t in range(tk // LANES)]
        m_tile = functools.reduce(jnp.maximum, tiles)
        m_prev = m_ref[hh]
        m_new = jnp.maximum(m_prev, jnp.max(m_tile, axis=-1, keepdims=True))
        alpha = jnp.exp2(m_prev - m_new)
        p_tiles = [jnp.exp2(t - m_new) for t in tiles]
        l_ref[hh] = alpha * l_ref[hh] + functools.reduce(jnp.add, p_tiles)
        m_ref[hh] = m_new
        return alpha, jnp.concatenate(p_tiles, axis=1).astype(BF16)

    def store(s_pair):
        for hh in range(2):
            s_ref[hh] = s_pair[hh]

    store(logits(0))

    def body(j, carry):
        pv = pending_pv(j)
        nxt = logits(j + 1)
        for hh in range(2):
            alpha, p = softmax(hh, False)
            p_ref[hh] = p
            acc_ref[hh] = alpha * (acc_ref[hh] + pv[hh])
        store(nxt)
        return carry

    lax.fori_loop(0, i, body, 0)
    pv = pending_pv(i)
    v = v_ref[0, pl.ds(pl.multiple_of(i * tk, tk), tk), :]
    outs = []
    for hh in range(2):
        alpha, p = softmax(hh, True)
        acc = alpha * (acc_ref[hh] + pv[hh]) + jnp.dot(p, v, preferred_element_type=F32)
        outs.append(acc / jnp.sum(l_ref[hh], axis=-1, keepdims=True))
    o_ref[0] = jnp.where(lo, outs[0], outs[1]).astype(BF16)


def _mixer_b(qkv_ab, f_cols):
    bsz, t, _ = qkv_ab.shape
    tq = TQ_B
    base = 3 * A_W // LANES
    npair = B_HEADS // 2
    return pl.pallas_call(
        _mixb_kernel,
        grid=(bsz, npair, t // tq),
        in_specs=[
            pl.BlockSpec((1, tq, LANES), lambda b, p, i: (b, i, base + p)),
            pl.BlockSpec((1, t, LANES), lambda b, p, i: (b, 0, base + npair + p)),
            pl.BlockSpec((1, t, LANES), lambda b, p, i: (b, 0, base + 2 * npair + p)),
            pl.BlockSpec((1, t, GATE_W), lambda b, p, i: (b, 0, 0)),
        ],
        out_specs=pl.BlockSpec((1, tq, LANES), lambda b, p, i: (b, i, p)),
        out_shape=jax.ShapeDtypeStruct((bsz, t, B_W), BF16),
        scratch_shapes=[
            pltpu.VMEM((2, tq, tq), F32),
            pltpu.VMEM((2, tq, tq), BF16),
            pltpu.VMEM((2, tq, LANES), F32),
            pltpu.VMEM((2, tq, LANES), F32),
            pltpu.VMEM((2, tq, LANES), F32),
        ],
        compiler_params=_params(("parallel", "parallel", "parallel")),
        name="mixer_b_forgetting_attn",
    )(qkv_ab, qkv_ab, qkv_ab, f_cols)


def _gdn_kernel(x_ref, gate_ref, z_ref, cw_ref, alog_ref, dtb_ref, gn_ref, o_ref,
                xs_ref, s_ref, vn_ref):
    nb, tm = x_ref.shape[0], x_ref.shape[1]
    sb = SB_GDN
    nsub = tm // sb
    cps = sb // CHUNK

    @pl.when(pl.program_id(1) == 0)
    def _():
        xs_ref[...] = jnp.zeros_like(xs_ref)
        s_ref[...] = jnp.zeros_like(s_ref)

    ri = lax.broadcasted_iota(jnp.int32, (sb, sb), 0)
    ci = lax.broadcasted_iota(jnp.int32, (sb, sb), 1)
    same = (ri // CHUNK) == (ci // CHUNK)
    tril = same & (ci <= ri)
    strict = same & (ci < ri)
    eye = (ri == ci).astype(F32)

    def level_mask(b):
        return ((ri // (2 * b)) == (ci // (2 * b))) & (((ri // b) % 2) == 1) & (((ci // b) % 2) == 0)

    pos = lax.broadcasted_iota(jnp.int32, (tm, GATE_W), 0) % CHUNK
    row8 = lax.broadcasted_iota(jnp.int32, (8, C3_W), 0)
    sub_chunk = lax.broadcasted_iota(jnp.int32, (sb, 1), 0) // CHUNK
    nt = (((1,), (1,)), ((), ()))
    tn = (((0,), (0,)), ((), ()))

    units = [(bb, sub, h) for bb in range(nb) for sub in range(nsub) for h in range(C_HEADS)]
    gates = {}
    qkvs = {}
    for bb in range(nb):
        xt = x_ref[bb]
        tail = xs_ref[bb]
        y = cw_ref[C_CONV - 1:C_CONV, :] * xt
        for k in range(1, C_CONV):
            head = jnp.where(row8 < k, pltpu.roll(tail, k, axis=0), pltpu.roll(xt[0:8], k, axis=0))
            shifted = jnp.concatenate([head, pltpu.roll(xt, k, axis=0)[8:]], axis=0)
            y = y + cw_ref[C_CONV - 1 - k:C_CONV - k, :] * shifted
        xs_ref[bb] = xt[tm - 8:tm]
        qkvs[bb] = _silu(y)

        gt = gate_ref[bb]
        beta = _sigmoid(gt)
        g = -jnp.exp(alog_ref[...]) * _softplus(gt + dtb_ref[...])
        gc = g
        d = 1
        while d < CHUNK:
            gc = gc + jnp.where(pos >= d, pltpu.roll(gc, d, axis=0), 0.0)
            d *= 2
        gl = jnp.concatenate(
            [jnp.broadcast_to(gc[(c + 1) * CHUNK - 1:(c + 1) * CHUNK, :], (CHUNK, GATE_W))
             for c in range(tm // CHUNK)], axis=0)
        gates[bb] = (beta, gc, gl, gc.T)

    work = {}
    for u in units:
        bb, sub, h = u
        rows = slice(sub * sb, (sub + 1) * sb)
        qkv = qkvs[bb]
        beta, gc, gl, gc_t = gates[bb]
        qh = qkv[rows, h * C_HEAD_DIM:(h + 1) * C_HEAD_DIM]
        kh = qkv[rows, C_W + h * C_HEAD_DIM:C_W + (h + 1) * C_HEAD_DIM]
        vh = qkv[rows, 2 * C_W + h * C_HEAD_DIM:2 * C_W + (h + 1) * C_HEAD_DIM]
        qh = qh * lax.rsqrt(jnp.sum(qh * qh, axis=-1, keepdims=True) + EPS) * (C_HEAD_DIM ** -0.5)
        kh = kh * lax.rsqrt(jnp.sum(kh * kh, axis=-1, keepdims=True) + EPS)
        b_col = beta[rows, BETA_LANE + h:BETA_LANE + h + 1]
        gc_col = gc[rows, A_LANE + h:A_LANE + h + 1]
        gl_col = gl[rows, A_LANE + h:A_LANE + h + 1]
        gc_row = gc_t[A_LANE + h:A_LANE + h + 1, rows]
        kbeta = kh * b_col
        lm = jnp.where(tril, jnp.exp(jnp.where(tril, gc_col - gc_row, 0.0)), 0.0)
        kq = lax.dot_general(jnp.concatenate([kbeta, qh], axis=0).astype(BF16), kh.astype(BF16),
                             nt, preferred_element_type=F32)
        amat = jnp.where(strict, kq[0:sb] * lm, 0.0)
        work[u] = dict(
            amat=amat, attn=jnp.where(tril, kq[sb:2 * sb] * lm, 0.0),
            rhs=jnp.concatenate([vh * b_col, kbeta * jnp.exp(gc_col)], axis=1).astype(BF16),
            qg=qh * jnp.exp(gc_col), kd=kh * jnp.exp(gl_col - gc_col), decay=jnp.exp(gl_col),
            tinv=eye - jnp.where(level_mask(1), amat, 0.0))

    b = 2
    while b < CHUNK:
        mask = level_mask(b)
        tas = {}
        for u in units:
            t16 = work[u]["tinv"].astype(BF16)
            work[u]["t16"] = t16
            tas[u] = jnp.dot(t16, jnp.where(mask, work[u]["amat"], 0.0).astype(BF16),
                             preferred_element_type=F32)
        for u in units:
            work[u]["tinv"] = work[u]["tinv"] - jnp.dot(tas[u].astype(BF16), work[u]["t16"],
                                                        preferred_element_type=F32)
        b *= 2
    for u in units:
        uw = jnp.dot(work[u]["tinv"].astype(BF16), work[u]["rhs"], preferred_element_type=F32)
        work[u]["u"] = uw[:, :C_HEAD_DIM]
        work[u]["w"] = uw[:, C_HEAD_DIM:]

    chains = [(bb, h) for bb in range(nb) for h in range(C_HEADS)]
    outs = {ch: [] for ch in chains}
    for sub in range(nsub):
        for bb, h in chains:
            vn_ref[bb * C_HEADS + h] = work[bb, sub, h]["u"]
        for cc in range(cps):
            rs = slice(cc * CHUNK, (cc + 1) * CHUNK)
            in_chunk = sub_chunk == cc
            first = {}
            for bb, h in chains:
                ph = work[bb, sub, h]
                s_mat = s_ref[bb, h]
                first[bb, h] = (s_mat, jnp.dot(
                    jnp.concatenate([ph["w"][rs], ph["qg"][rs]], axis=0).astype(BF16),
                    s_mat.astype(BF16), preferred_element_type=F32))
            for bb, h in chains:
                ph = work[bb, sub, h]
                s_mat, ws = first[bb, h]
                vn = vn_ref.at[bb * C_HEADS + h]
                vn[rs, :] = ph["u"][rs] - ws[0:CHUNK]
                vn_all = vn[...].astype(BF16)
                outs[bb, h].append(ws[CHUNK:] + jnp.dot(ph["attn"][rs].astype(BF16), vn_all,
                                                        preferred_element_type=F32))
                kd = jnp.where(in_chunk, ph["kd"], 0.0).astype(BF16)
                s_ref[bb, h] = s_mat * ph["decay"][cc * CHUNK:cc * CHUNK + 1] + lax.dot_general(
                    kd, vn_all, tn, preferred_element_type=F32)

    for bb in range(nb):
        for h in range(C_HEADS):
            hs = slice(h * C_HEAD_DIM, (h + 1) * C_HEAD_DIM)
            o = jnp.concatenate(outs[bb, h], axis=0)
            o_ref[bb, :, hs] = (_rms(o, gn_ref[...]) * _silu(z_ref[bb, :, hs])).astype(BF16)


def _mixer_c(qkv_c, gates, z, conv_w, alog_lane, dtb_lane, gnorm):
    bsz, t, _ = qkv_c.shape
    tm = TM_GDN
    nb = NB_GDN if bsz % NB_GDN == 0 else 1
    row = lambda b, i: (b, i, 0)
    return pl.pallas_call(
        _gdn_kernel,
        grid=(bsz // nb, t // tm),
        in_specs=[
            pl.BlockSpec((nb, tm, C3_W), row),
            pl.BlockSpec((nb, tm, GATE_W), row),
            pl.BlockSpec((nb, tm, C_W), row),
            _const_spec((C_CONV, C3_W)),
            _const_spec((1, GATE_W)),
            _const_spec((1, GATE_W)),
            _const_spec((1, C_HEAD_DIM)),
        ],
        out_specs=pl.BlockSpec((nb, tm, C_W), row),
        out_shape=jax.ShapeDtypeStruct((bsz, t, C_W), BF16),
        scratch_shapes=[
            pltpu.VMEM((nb, 8, C3_W), F32),
            pltpu.VMEM((nb, C_HEADS, C_HEAD_DIM, C_HEAD_DIM), F32),
            pltpu.VMEM((nb * C_HEADS, SB_GDN, C_HEAD_DIM), F32),
        ],
        compiler_params=_params(("parallel", "arbitrary")),
        name="mixer_c_gated_deltanet",
    )(qkv_c, gates, z, conv_w, alog_lane, dtb_lane, gnorm)


def _ffn_kernel(x_ref, oa_ref, ob_ref, oc_ref, mod_ref, wo_ref, g_ref, wup_ref, cw_ref, wd_ref,
                fg_ref, o_ref,
                hb_ref, acc_ref, ug_ref, uu_ref, act_ref, hg_ref, hu_ref, *, final):
    tm = x_ref.shape[1]

    @pl.when(pl.program_id(1) == 0)
    def _():
        hg_ref[...] = jnp.zeros_like(hg_ref)
        hu_ref[...] = jnp.zeros_like(hu_ref)

    gate1 = mod_ref[0, 2:3, :]
    shift2 = mod_ref[0, 3:4, :]
    scale2 = mod_ref[0, 4:5, :]
    gate2 = mod_ref[0, 5:6, :]
    mix = (jnp.dot(oa_ref[0], wo_ref[0:A_W, :], preferred_element_type=F32)
           + jnp.dot(ob_ref[0], wo_ref[A_W:A_W + B_W, :], preferred_element_type=F32)
           + jnp.dot(oc_ref[0], wo_ref[A_W + B_W:, :], preferred_element_type=F32))
    x1 = x_ref[0] + gate1 * mix
    h = _rms(x1, g_ref[...]) * (1.0 + scale2) + shift2
    hb_ref[...] = h.astype(BF16)
    acc_ref[...] = jnp.zeros_like(acc_ref)

    def gate_cols(c):
        return slice(c * FF_CH, (c + 1) * FF_CH)

    def up_cols(c):
        return slice(D_FF + c * FF_CH, D_FF + (c + 1) * FF_CH)

    def up_proj(c):
        hb = hb_ref[...]
        ug_ref[c % 2, 8:8 + tm, :] = jnp.dot(hb, wup_ref[:, gate_cols(c)], preferred_element_type=F32)
        uu_ref[c % 2, 8:8 + tm, :] = jnp.dot(hb, wup_ref[:, up_cols(c)], preferred_element_type=F32)

    def conv(u_ref, hdr_ref, c, w):
        u = u_ref.at[c % 2]
        u[0:8, :] = hdr_ref[c]
        y = w[0:1, :] * u[6:6 + tm, :] + w[1:2, :] * u[7:7 + tm, :] + w[2:3, :] * u[8:8 + tm, :]
        hdr_ref[c] = u[tm:tm + 8, :]
        return y

    up_proj(0)
    for c in range(N_FF_CH + 1):
        if c + 1 < N_FF_CH:
            up_proj(c + 1)
        if c >= 1:
            acc_ref[...] += jnp.dot(act_ref[(c - 1) % 2], wd_ref[gate_cols(c - 1), :],
                                    preferred_element_type=F32)
        if c < N_FF_CH:
            gate = conv(ug_ref, hg_ref, c, cw_ref[:, gate_cols(c)])
            up = conv(uu_ref, hu_ref, c, cw_ref[:, up_cols(c)])
            act_ref[c % 2] = (_silu(gate) * up).astype(BF16)
    x2 = x1 + gate2 * acc_ref[...]
    if final:
        x2 = _rms(x2, fg_ref[...])
    o_ref[0] = x2


def _out_ffn(x, o_a, o_b, o_c, mod_l, w_out, g, w_up, conv_w, w_down, final_g, final):
    bsz, t, _ = x.shape
    tm = TM_FFN
    row = lambda b, i: (b, i, 0)
    return pl.pallas_call(
        functools.partial(_ffn_kernel, final=final),
        grid=(bsz, t // tm),
        in_specs=[
            pl.BlockSpec((1, tm, D_MODEL), row),
            pl.BlockSpec((1, tm, A_W), row),
            pl.BlockSpec((1, tm, B_W), row),
            pl.BlockSpec((1, tm, C_W), row),
            pl.BlockSpec((1, 6, D_MODEL), lambda b, i: (b, 0, 0)),
            _const_spec((D_MODEL, D_MODEL)),
            _const_spec((1, D_MODEL)),
            _const_spec((D_MODEL, 2 * D_FF)),
            _const_spec((FFN_CONV, 2 * D_FF)),
            _const_spec((D_FF, D_MODEL)),
            _const_spec((1, D_MODEL)),
        ],
        out_specs=pl.BlockSpec((1, tm, D_MODEL), row),
        out_shape=jax.ShapeDtypeStruct((bsz, t, D_MODEL), F32),
        scratch_shapes=[
            pltpu.VMEM((tm, D_MODEL), BF16),
            pltpu.VMEM((tm, D_MODEL), F32),
            pltpu.VMEM((2, tm + 8, FF_CH), F32),
            pltpu.VMEM((2, tm + 8, FF_CH), F32),
            pltpu.VMEM((2, tm, FF_CH), BF16),
            pltpu.VMEM((N_FF_CH, 8, FF_CH), F32),
            pltpu.VMEM((N_FF_CH, 8, FF_CH), F32),
        ],
        compiler_params=_params(("parallel", "arbitrary")),
        name="outproj_convmlp",
    )(x, o_a, o_b, o_c, mod_l, w_out, g, w_up, conv_w, w_down, final_g)


def _lane_vec(values, start):
    return jnp.zeros((1, GATE_W), F32).at[0, start:start + values.shape[0]].set(values)


def kernel(x, c, ada_w, ada_b, norm_mix_g, norm_ffn_g, w_in, w_out, rel_bias, fgate_bias,
           gdn_conv_w, gdn_A_log, gdn_dt_bias, gdn_norm_g, ffn_w_up, ffn_conv_w,
           ffn_w_down, final_norm_g):
    mod = _modulation(c, ada_w, ada_b)
    final_g = final_norm_g.reshape(1, D_MODEL)
    bias_tiles = _rel_bias_tiles(rel_bias)
    for l in range(DEPTH):
        qkv_ab, qkv_c, z, gates = _inproj(
            x, mod[l], norm_mix_g[l].reshape(1, D_MODEL), _arrange_w_in(w_in[l]))
        f_cols = _fgate(gates, _lane_vec(fgate_bias[l], F_LANE))
        o_a = _mixer_a(qkv_ab, bias_tiles[l])
        o_b = _mixer_b(qkv_ab, f_cols)
        o_c = _mixer_c(qkv_c, gates, z, gdn_conv_w[l],
                       _lane_vec(gdn_A_log[l], A_LANE), _lane_vec(gdn_dt_bias[l], A_LANE),
                       gdn_norm_g[l].reshape(1, C_HEAD_DIM))
        x = _out_ffn(
            x, o_a, o_b, o_c, mod[l], w_out[l].astype(BF16),
            norm_ffn_g[l].reshape(1, D_MODEL),
            ffn_w_up[l].astype(BF16), ffn_conv_w[l], ffn_w_down[l].astype(BF16),
            final_g, final=(l == DEPTH - 1))
    return x
```

```python
import functools

import jax
import jax.numpy as jnp
from jax import lax
from jax.experimental import pallas as pl
from jax.experimental.pallas import tpu as pltpu

F32 = jnp.float32
BF16 = jnp.bfloat16

D_MODEL = 1024
DEPTH = 2
CHUNK = 64
EPS = 1e-6
A_HEADS = 4
A_HEAD_DIM = 64
A_LEFT_CHUNKS = 8
REL_PAST = 256
REL_FUTURE = CHUNK - 1
B_HEADS = 4
B_HEAD_DIM = 64
C_HEADS = 4
C_HEAD_DIM = 128
C_CONV = 4
D_FF = 2816
FFN_CONV = 3
A_W = A_HEADS * A_HEAD_DIM
B_W = B_HEADS * B_HEAD_DIM
C_W = C_HEADS * C_HEAD_DIM
AB_W = 3 * A_W + 3 * B_W
C3_W = 3 * C_W
LANES = 128
GATE_W = LANES
F_LANE, BETA_LANE, A_LANE = 0, 4, 8
NEG = -1e30
LOG2E = 1.4426950408889634
F_PARTS = 3
VMEM_LIMIT = 56 * 1024 * 1024

TM_PROJ = 512
TF_GATE = 1024
TQ_A = 512
TQ_B = 512
TM_GDN = 256
SB_GDN = 128
NB_GDN = 2
TM_FFN = 512
FF_CH = 1280
N_FF_CH = -(-D_FF // FF_CH)


def _sigmoid(x):
    return 1.0 / (1.0 + jnp.exp(-x))


def _silu(x):
    return x * _sigmoid(x)


def _softplus(x):
    return jnp.maximum(x, 0.0) + jnp.log1p(jnp.exp(-jnp.abs(x)))


def _log_sigmoid(x):
    return jnp.minimum(x, 0.0) - jnp.log1p(jnp.exp(-jnp.abs(x)))


def _rms(x, g):
    return x * lax.rsqrt(jnp.mean(x * x, axis=-1, keepdims=True) + EPS) * g


def _params(sem):
    return pltpu.CompilerParams(dimension_semantics=sem, vmem_limit_bytes=VMEM_LIMIT)


def _const_spec(shape):
    nd = len(shape)
    return pl.BlockSpec(shape, lambda *_: (0,) * nd, pipeline_mode=pl.Buffered(1))


def _mod_kernel(c_ref, w_ref, b_ref, o_ref):
    c_act = _silu(c_ref[...])
    o_ref[0] = jnp.dot(c_act, w_ref[0], preferred_element_type=F32) + b_ref[0]


def _modulation(c, ada_w, ada_b):
    bsz = c.shape[0]
    rows = 8
    cp = jnp.zeros((rows, D_MODEL), F32).at[:bsz].set(c)
    ncol = 6 * D_MODEL
    cb = 1536
    out = pl.pallas_call(
        _mod_kernel,
        grid=(DEPTH, ncol // cb),
        in_specs=[
            pl.BlockSpec((rows, D_MODEL), lambda l, j: (0, 0)),
            pl.BlockSpec((1, D_MODEL, cb), lambda l, j: (l, 0, j)),
            pl.BlockSpec((1, 1, cb), lambda l, j: (l, 0, j)),
        ],
        out_specs=pl.BlockSpec((1, rows, cb), lambda l, j: (l, 0, j)),
        out_shape=jax.ShapeDtypeStruct((DEPTH, rows, ncol), F32),
        compiler_params=_params(("arbitrary", "arbitrary")),
        name="adaln_modulation",
    )(cp, ada_w, ada_b.reshape(DEPTH, 1, ncol))
    return out[:, :bsz].reshape(DEPTH, bsz, 6, D_MODEL)


_PROJ_W = AB_W + C3_W + C_W + GATE_W


def _inproj_kernel(x_ref, mod_ref, g_ref, w_ref, oab_ref, oc_ref, oz_ref, og_ref):
    x = x_ref[0]
    shift = mod_ref[0, 0:1, :]
    scale = mod_ref[0, 1:2, :]
    h = _rms(x, g_ref[...]) * (1.0 + scale) + shift
    hb = h.astype(BF16)
    cw = 512
    for j in range(AB_W // cw):
        oab_ref[0, :, j * cw:(j + 1) * cw] = jnp.dot(
            hb, w_ref[:, j * cw:(j + 1) * cw], preferred_element_type=F32).astype(BF16)
    for j in range(C3_W // cw):
        oc_ref[0, :, j * cw:(j + 1) * cw] = jnp.dot(
            hb, w_ref[:, AB_W + j * cw:AB_W + (j + 1) * cw], preferred_element_type=F32)
    oz_ref[0] = jnp.dot(hb, w_ref[:, AB_W + C3_W:AB_W + C3_W + C_W],
                        preferred_element_type=F32)
    og_ref[0] = jnp.dot(hb, w_ref[:, AB_W + C3_W + C_W:], preferred_element_type=F32)


def _inproj(x, mod_l, g, w):
    bsz, t, _ = x.shape
    tm = TM_PROJ
    row = lambda b, i: (b, i, 0)
    return pl.pallas_call(
        _inproj_kernel,
        grid=(bsz, t // tm),
        in_specs=[
            pl.BlockSpec((1, tm, D_MODEL), row),
            pl.BlockSpec((1, 6, D_MODEL), lambda b, i: (b, 0, 0)),
            _const_spec((1, D_MODEL)),
            _const_spec((D_MODEL, _PROJ_W)),
        ],
        out_specs=[
            pl.BlockSpec((1, tm, AB_W), row),
            pl.BlockSpec((1, tm, C3_W), row),
            pl.BlockSpec((1, tm, C_W), row),
            pl.BlockSpec((1, tm, GATE_W), row),
        ],
        out_shape=[
            jax.ShapeDtypeStruct((bsz, t, AB_W), BF16),
            jax.ShapeDtypeStruct((bsz, t, C3_W), F32),
            jax.ShapeDtypeStruct((bsz, t, C_W), F32),
            jax.ShapeDtypeStruct((bsz, t, GATE_W), F32),
        ],
        compiler_params=_params(("parallel", "parallel")),
        name="inproj",
    )(x, mod_l, g, w)


def _arrange_w_in(w_in_l):
    a_end = 3 * A_W
    b_end = a_end + 3 * B_W
    f_end = b_end + B_HEADS
    c_end = f_end + C3_W
    beta_end = c_end + C_HEADS
    a_gate_end = beta_end + C_HEADS
    qa_scale = A_HEAD_DIM ** -0.5
    qb_scale = B_HEAD_DIM ** -0.5 * LOG2E
    w_ab = jnp.concatenate([
        w_in_l[:, 0:A_W] * qa_scale, w_in_l[:, A_W:a_end],
        w_in_l[:, a_end:a_end + B_W] * qb_scale, w_in_l[:, a_end + B_W:b_end]], axis=1)
    w_gate = jnp.concatenate([
        w_in_l[:, b_end:f_end], w_in_l[:, c_end:beta_end], w_in_l[:, beta_end:a_gate_end],
        jnp.zeros((D_MODEL, GATE_W - B_HEADS - 2 * C_HEADS), F32)], axis=1)
    w = jnp.concatenate([w_ab, w_in_l[:, f_end:c_end], w_in_l[:, a_gate_end:], w_gate], axis=1)
    return w.astype(BF16)


def _fgate_kernel(g_ref, fb_ref, o_ref, carry_ref):
    @pl.when(pl.program_id(1) == 0)
    def _():
        carry_ref[...] = jnp.zeros_like(carry_ref)

    tf = g_ref.shape[1]
    lane = lax.broadcasted_iota(jnp.int32, (tf, GATE_W), 1)
    row = lax.broadcasted_iota(jnp.int32, (tf, GATE_W), 0)
    cs = jnp.where(lane < B_HEADS, _log_sigmoid(g_ref[0] + fb_ref[...]), 0.0)
    d = 1
    while d < tf:
        cs = cs + jnp.where(row >= d, pltpu.roll(cs, d, axis=0), 0.0)
        d *= 2
    cs = cs + carry_ref[0:1, :]
    carry_ref[...] = jnp.broadcast_to(cs[tf - 1:tf, :], carry_ref.shape)
    rest = cs * (-LOG2E)
    out = jnp.zeros_like(rest)
    for i in range(F_PARTS):
        part = rest.astype(BF16).astype(F32)
        rest = rest - part
        out = out + (part if i == 0 else pltpu.roll(part, i * B_HEADS, axis=1))
    o_ref[0] = out.astype(BF16)


def _fgate(gates, fbias_lane):
    bsz, t, _ = gates.shape
    tf = min(TF_GATE, t)
    return pl.pallas_call(
        _fgate_kernel,
        grid=(bsz, t // tf),
        in_specs=[
            pl.BlockSpec((1, tf, GATE_W), lambda b, i: (b, i, 0)),
            _const_spec((1, GATE_W)),
        ],
        out_specs=pl.BlockSpec((1, tf, GATE_W), lambda b, i: (b, i, 0)),
        out_shape=jax.ShapeDtypeStruct((bsz, t, GATE_W), BF16),
        scratch_shapes=[pltpu.VMEM((8, LANES), F32)],
        compiler_params=_params(("parallel", "arbitrary")),
        name="fgate_cumsum",
    )(gates, fbias_lane)


def _mixa_kernel(q_ref, kp_ref, kc_ref, vp_ref, vc_ref, bias_ref, o_ref):
    i = pl.program_id(1)
    tq = q_ref.shape[1]
    lane = lax.broadcasted_iota(jnp.int32, (1, LANES), 1)
    lo = lane < A_HEAD_DIM
    has_prev = i > 0
    nt = (((1,), (1,)), ((), ()))
    logits = []
    for h in range(A_HEADS):
        sl = slice((h // 2) * LANES, (h // 2 + 1) * LANES)
        q = q_ref[0, :, sl]
        qh = jnp.where(lo if h % 2 == 0 else ~lo, q, jnp.zeros_like(q))
        logits.append((lax.dot_general(qh, kp_ref[0, :, sl], nt, preferred_element_type=F32),
                       lax.dot_general(qh, kc_ref[0, :, sl], nt, preferred_element_type=F32)))
    outs = []
    for h in range(A_HEADS):
        sl = slice((h // 2) * LANES, (h // 2 + 1) * LANES)
        s_p = jnp.where(has_prev, logits[h][0] + bias_ref[h, :, 0:tq], NEG)
        s_c = logits[h][1] + bias_ref[h, :, tq:2 * tq]
        m = jnp.maximum(jnp.max(s_p, axis=-1, keepdims=True), jnp.max(s_c, axis=-1, keepdims=True))
        p_p = jnp.exp(s_p - m)
        p_c = jnp.exp(s_c - m)
        l = jnp.sum(p_p, axis=-1, keepdims=True) + jnp.sum(p_c, axis=-1, keepdims=True)
        o = (jnp.dot(p_p.astype(BF16), vp_ref[0, :, sl], preferred_element_type=F32)
             + jnp.dot(p_c.astype(BF16), vc_ref[0, :, sl], preferred_element_type=F32))
        outs.append(o / l)
    for pair in range(A_HEADS // 2):
        o_ref[0, :, pair * LANES:(pair + 1) * LANES] = jnp.where(
            lo, outs[2 * pair], outs[2 * pair + 1]).astype(BF16)


def _mixer_a(qkv_ab, bias):
    bsz, t, _ = qkv_ab.shape
    tq = TQ_A
    prev = lambda col: (lambda b, i: (b, jnp.maximum(i - 1, 0), col))
    cur = lambda col: (lambda b, i: (b, i, col))
    blk = (1, tq, A_W)
    return pl.pallas_call(
        _mixa_kernel,
        grid=(bsz, t // tq),
        in_specs=[
            pl.BlockSpec(blk, cur(0)),
            pl.BlockSpec(blk, prev(1)), pl.BlockSpec(blk, cur(1)),
            pl.BlockSpec(blk, prev(2)), pl.BlockSpec(blk, cur(2)),
            _const_spec((A_HEADS, tq, 2 * tq)),
        ],
        out_specs=pl.BlockSpec(blk, cur(0)),
        out_shape=jax.ShapeDtypeStruct((bsz, t, A_W), BF16),
        compiler_params=_params(("parallel", "parallel")),
        name="mixer_a_chunk_attn",
    )(qkv_ab, qkv_ab, qkv_ab, qkv_ab, qkv_ab, bias)


_REL_PAD = 384
_REL_ROLL_W = 4 * TQ_A


def _relbias_kernel(tab_ref, o_ref, row_ref):
    tq = o_ref.shape[1]
    m = lax.broadcasted_iota(jnp.int32, (_REL_PAD, _REL_ROLL_W), 1)
    m = jnp.where(m < _REL_ROLL_W // 2, m, m - _REL_ROLL_W)
    idx = jnp.clip(tq - m, -REL_FUTURE, REL_PAST) + REL_FUTURE
    n = lax.broadcasted_iota(jnp.int32, (_REL_PAD, _REL_ROLL_W), 0)
    onehot = (n == idx).astype(F32)
    row_ref[...] = jnp.dot(tab_ref[...], onehot, precision=lax.Precision.HIGHEST,
                           preferred_element_type=F32)
    row = row_ref[pl.ds(pl.program_id(0), 1), :]
    rolled = pltpu.roll(jnp.broadcast_to(row, (tq, _REL_ROLL_W)), 0, 1, stride=1, stride_axis=0)
    qc = lax.broadcasted_iota(jnp.int32, (tq, 2 * tq), 0) // CHUNK
    kc = lax.broadcasted_iota(jnp.int32, (tq, 2 * tq), 1) // CHUNK - tq // CHUNK
    valid = (kc <= qc) & (kc >= qc - A_LEFT_CHUNKS)
    o_ref[0] = jnp.where(valid, rolled[:, 0:2 * tq], NEG)


def _rel_bias_tiles(rel_bias):
    nrel = rel_bias.shape[-1]
    nrow = DEPTH * A_HEADS
    tab = jnp.zeros((nrow, _REL_PAD), F32).at[:, :nrel].set(rel_bias.reshape(nrow, nrel))
    out = pl.pallas_call(
        _relbias_kernel,
        grid=(nrow,),
        in_specs=[pl.BlockSpec((nrow, _REL_PAD), lambda h: (0, 0))],
        out_specs=pl.BlockSpec((1, TQ_A, 2 * TQ_A), lambda h: (h, 0, 0)),
        out_shape=jax.ShapeDtypeStruct((nrow, TQ_A, 2 * TQ_A), F32),
        scratch_shapes=[pltpu.VMEM((nrow, _REL_ROLL_W), F32)],
        compiler_params=_params(("arbitrary",)),
        name="relbias_toeplitz",
    )(tab)
    return out.reshape(DEPTH, A_HEADS, TQ_A, 2 * TQ_A)


def _mixb_kernel(q_ref, k_ref, v_ref, f_ref, o_ref, s_ref, p_ref, m_ref, l_ref, acc_ref):
    pair = pl.program_id(1)
    i = pl.program_id(2)
    tq = q_ref.shape[1]
    tk = tq
    lane = lax.broadcasted_iota(jnp.int32, (1, LANES), 1)
    lo = lane < B_HEAD_DIM
    q = q_ref[0]
    zero = jnp.zeros_like(q)
    qs = []
    for hh in range(2):
        head = 2 * pair + hh
        picks = functools.reduce(jnp.logical_or, [lane == head + t * B_HEADS for t in range(F_PARTS)])
        ones = jnp.broadcast_to(jnp.where(picks, 1.0, 0.0).astype(F32), (tq, LANES)).astype(BF16)
        qs.append(jnp.concatenate([jnp.where(lo if hh == 0 else ~lo, q, zero), ones], axis=1))
    nt = (((1,), (1,)), ((), ()))
    m_ref[...] = jnp.full(m_ref.shape, NEG, F32)
    l_ref[...] = jnp.zeros(l_ref.shape, F32)
    acc_ref[...] = jnp.zeros(acc_ref.shape, F32)
    p_ref[...] = jnp.zeros(p_ref.shape, BF16)

    def logits(j):
        off = pl.multiple_of(j * tk, tk)
        kf = jnp.concatenate([k_ref[0, pl.ds(off, tk), :], f_ref[0, pl.ds(off, tk), :]], axis=1)
        return [lax.dot_general(qs[hh], kf, nt, preferred_element_type=F32) for hh in range(2)]

    def pending_pv(j):
        off = pl.multiple_of(jnp.maximum(j - 1, 0) * tk, tk)
        v = v_ref[0, pl.ds(off, tk), :]
        return [jnp.dot(p_ref[hh], v, preferred_element_type=F32) for hh in range(2)]

    def softmax(hh, diagonal):
        s = s_ref[hh]
        if diagonal:
            r = lax.broadcasted_iota(jnp.int32, (tq, tk), 0)
            c = lax.broadcasted_iota(jnp.int32, (tq, tk), 1)
            s = jnp.where(c <= r, s, NEG)
        tiles = [s[:, t * LANES:(t + 1) * LANES] for t in range(tk // LANES)]
        m_tile = functools.reduce(jnp.maximum, tiles)
        m_prev = m_ref[hh]
        m_new = jnp.maximum(m_prev, jnp.max(m_tile, axis=-1, keepdims=True))
        alpha = jnp.exp2(m_prev - m_new)
        p_tiles = [jnp.exp2(t - m_new) for t in tiles]
        l_ref[hh] = alpha * l_ref[hh] + functools.reduce(jnp.add, p_tiles)
        m_ref[hh] = m_new
        return alpha, jnp.concatenate(p_tiles, axis=1).astype(BF16)

    def store(s_pair):
        for hh in range(2):
            s_ref[hh] = s_pair[hh]

    store(logits(0))

    def body(j, carry):
        pv = pending_pv(j)
        nxt = logits(j + 1)
        for hh in range(2):
            alpha, p = softmax(hh, False)
            p_ref[hh] = p
            acc_ref[hh] = alpha * (acc_ref[hh] + pv[hh])
        store(nxt)
        return carry

    lax.fori_loop(0, i, body, 0)
    pv = pending_pv(i)
    v = v_ref[0, pl.ds(pl.multiple_of(i * tk, tk), tk), :]
    outs = []
    for hh in range(2):
        alpha, p = softmax(hh, True)
        acc = alpha * (acc_ref[hh] + pv[hh]) + jnp.dot(p, v, preferred_element_type=F32)
        outs.append(acc / jnp.sum(l_ref[hh], axis=-1, keepdims=True))
    o_ref[0] = jnp.where(lo, outs[0], outs[1]).astype(BF16)


def _mixer_b(qkv_ab, f_cols):
    bsz, t, _ = qkv_ab.shape
    tq = TQ_B
    base = 3 * A_W // LANES
    npair = B_HEADS // 2
    return pl.pallas_call(
        _mixb_kernel,
        grid=(bsz, npair, t // tq),
        in_specs=[
            pl.BlockSpec((1, tq, LANES), lambda b, p, i: (b, i, base + p)),
            pl.BlockSpec((1, t, LANES), lambda b, p, i: (b, 0, base + npair + p)),
            pl.BlockSpec((1, t, LANES), lambda b, p, i: (b, 0, base + 2 * npair + p)),
            pl.BlockSpec((1, t, GATE_W), lambda b, p, i: (b, 0, 0)),
        ],
        out_specs=pl.BlockSpec((1, tq, LANES), lambda b, p, i: (b, i, p)),
        out_shape=jax.ShapeDtypeStruct((bsz, t, B_W), BF16),
        scratch_shapes=[
            pltpu.VMEM((2, tq, tq), F32),
            pltpu.VMEM((2, tq, tq), BF16),
            pltpu.VMEM((2, tq, LANES), F32),
            pltpu.VMEM((2, tq, LANES), F32),
            pltpu.VMEM((2, tq, LANES), F32),
        ],
        compiler_params=_params(("parallel", "parallel", "parallel")),
        name="mixer_b_forgetting_attn",
    )(qkv_ab, qkv_ab, qkv_ab, f_cols)


def _gdn_kernel(x_ref, gate_ref, z_ref, cw_ref, alog_ref, dtb_ref, gn_ref, o_ref,
                xs_ref, s_ref, vn_ref):
    nb, tm = x_ref.shape[0], x_ref.shape[1]
    sb = SB_GDN
    nsub = tm // sb
    cps = sb // CHUNK

    @pl.when(pl.program_id(1) == 0)
    def _():
        xs_ref[...] = jnp.zeros_like(xs_ref)
        s_ref[...] = jnp.zeros_like(s_ref)

    ri = lax.broadcasted_iota(jnp.int32, (sb, sb), 0)
    ci = lax.broadcasted_iota(jnp.int32, (sb, sb), 1)
    same = (ri // CHUNK) == (ci // CHUNK)
    tril = same & (ci <= ri)
    strict = same & (ci < ri)
    eye = (ri == ci).astype(F32)

    def level_mask(b):
        return ((ri // (2 * b)) == (ci // (2 * b))) & (((ri // b) % 2) == 1) & (((ci // b) % 2) == 0)

    pos = lax.broadcasted_iota(jnp.int32, (tm, GATE_W), 0) % CHUNK
    row8 = lax.broadcasted_iota(jnp.int32, (8, C3_W), 0)
    sub_chunk = lax.broadcasted_iota(jnp.int32, (sb, 1), 0) // CHUNK
    nt = (((1,), (1,)), ((), ()))
    tn = (((0,), (0,)), ((), ()))

    units = [(bb, sub, h) for bb in range(nb) for sub in range(nsub) for h in range(C_HEADS)]
    gates = {}
    qkvs = {}
    for bb in range(nb):
        xt = x_ref[bb]
        tail = xs_ref[bb]
        y = cw_ref[C_CONV - 1:C_CONV, :] * xt
        for k in range(1, C_CONV):
            head = jnp.where(row8 < k, pltpu.roll(tail, k, axis=0), pltpu.roll(xt[0:8], k, axis=0))
            shifted = jnp.concatenate([head, pltpu.roll(xt, k, axis=0)[8:]], axis=0)
            y = y + cw_ref[C_CONV - 1 - k:C_CONV - k, :] * shifted
        xs_ref[bb] = xt[tm - 8:tm]
        qkvs[bb] = _silu(y)

        gt = gate_ref[bb]
        beta = _sigmoid(gt)
        g = -jnp.exp(alog_ref[...]) * _softplus(gt + dtb_ref[...])
        gc = g
        d = 1
        while d < CHUNK:
            gc = gc + jnp.where(pos >= d, pltpu.roll(gc, d, axis=0), 0.0)
            d *= 2
        gl = jnp.concatenate(
            [jnp.broadcast_to(gc[(c + 1) * CHUNK - 1:(c + 1) * CHUNK, :], (CHUNK, GATE_W))
             for c in range(tm // CHUNK)], axis=0)
        gates[bb] = (beta, gc, gl, gc.T)

    work = {}
    for u in units:
        bb, sub, h = u
        rows = slice(sub * sb, (sub + 1) * sb)
        qkv = qkvs[bb]
        beta, gc, gl, gc_t = gates[bb]
        qh = qkv[rows, h * C_HEAD_DIM:(h + 1) * C_HEAD_DIM]
        kh = qkv[rows, C_W + h * C_HEAD_DIM:C_W + (h + 1) * C_HEAD_DIM]
        vh = qkv[rows, 2 * C_W + h * C_HEAD_DIM:2 * C_W + (h + 1) * C_HEAD_DIM]
        qh = qh * lax.rsqrt(jnp.sum(qh * qh, axis=-1, keepdims=True) + EPS) * (C_HEAD_DIM ** -0.5)
        kh = kh * lax.rsqrt(jnp.sum(kh * kh, axis=-1, keepdims=True) + EPS)
        b_col = beta[rows, BETA_LANE + h:BETA_LANE + h + 1]
        gc_col = gc[rows, A_LANE + h:A_LANE + h + 1]
        gl_col = gl[rows, A_LANE + h:A_LANE + h + 1]
        gc_row = gc_t[A_LANE + h:A_LANE + h + 1, rows]
        kbeta = kh * b_col
        lm = jnp.where(tril, jnp.exp(jnp.where(tril, gc_col - gc_row, 0.0)), 0.0)
        kq = lax.dot_general(jnp.concatenate([kbeta, qh], axis=0).astype(BF16), kh.astype(BF16),
                             nt, preferred_element_type=F32)
        amat = jnp.where(strict, kq[0:sb] * lm, 0.0)
        work[u] = dict(
            amat=amat, attn=jnp.where(tril, kq[sb:2 * sb] * lm, 0.0),
            rhs=jnp.concatenate([vh * b_col, kbeta * jnp.exp(gc_col)], axis=1).astype(BF16),
            qg=qh * jnp.exp(gc_col), kd=kh * jnp.exp(gl_col - gc_col), decay=jnp.exp(gl_col),
            tinv=eye - jnp.where(level_mask(1), amat, 0.0))

    b = 2
    while b < CHUNK:
        mask = level_mask(b)
        tas = {}
        for u in units:
            t16 = work[u]["tinv"].astype(BF16)
            work[u]["t16"] = t16
            tas[u] = jnp.dot(t16, jnp.where(mask, work[u]["amat"], 0.0).astype(BF16),
                             preferred_element_type=F32)
        for u in units:
            work[u]["tinv"] = work[u]["tinv"] - jnp.dot(tas[u].astype(BF16), work[u]["t16"],
                                                        preferred_element_type=F32)
        b *= 2
    for u in units:
        uw = jnp.dot(work[u]["tinv"].astype(BF16), work[u]["rhs"], preferred_element_type=F32)
        work[u]["u"] = uw[:, :C_HEAD_DIM]
        work[u]["w"] = uw[:, C_HEAD_DIM:]

    chains = [(bb, h) for bb in range(nb) for h in range(C_HEADS)]
    outs = {ch: [] for ch in chains}
    for sub in range(nsub):
        for bb, h in chains:
            vn_ref[bb * C_HEADS + h] = work[bb, sub, h]["u"]
        for cc in range(cps):
            rs = slice(cc * CHUNK, (cc + 1) * CHUNK)
            in_chunk = sub_chunk == cc
            first = {}
            for bb, h in chains:
                ph = work[bb, sub, h]
                s_mat = s_ref[bb, h]
                first[bb, h] = (s_mat, jnp.dot(
                    jnp.concatenate([ph["w"][rs], ph["qg"][rs]], axis=0).astype(BF16),
                    s_mat.astype(BF16), preferred_element_type=F32))
            for bb, h in chains:
                ph = work[bb, sub, h]
                s_mat, ws = first[bb, h]
                vn = vn_ref.at[bb * C_HEADS + h]
                vn[rs, :] = ph["u"][rs] - ws[0:CHUNK]
                vn_all = vn[...].astype(BF16)
                outs[bb, h].append(ws[CHUNK:] + jnp.dot(ph["attn"][rs].astype(BF16), vn_all,
                                                        preferred_element_type=F32))
                kd = jnp.where(in_chunk, ph["kd"], 0.0).astype(BF16)
                s_ref[bb, h] = s_mat * ph["decay"][cc * CHUNK:cc * CHUNK + 1] + lax.dot_general(
                    kd, vn_all, tn, preferred_element_type=F32)

    for bb in range(nb):
        for h in range(C_HEADS):
            hs = slice(h * C_HEAD_DIM, (h + 1) * C_HEAD_DIM)
            o = jnp.concatenate(outs[bb, h], axis=0)
            o_ref[bb, :, hs] = (_rms(o, gn_ref[...]) * _silu(z_ref[bb, :, hs])).astype(BF16)


def _mixer_c(qkv_c, gates, z, conv_w, alog_lane, dtb_lane, gnorm):
    bsz, t, _ = qkv_c.shape
    tm = TM_GDN
    nb = NB_GDN if bsz % NB_GDN == 0 else 1
    row = lambda b, i: (b, i, 0)
    return pl.pallas_call(
        _gdn_kernel,
        grid=(bsz // nb, t // tm),
        in_specs=[
            pl.BlockSpec((nb, tm, C3_W), row),
            pl.BlockSpec((nb, tm, GATE_W), row),
            pl.BlockSpec((nb, tm, C_W), row),
            _const_spec((C_CONV, C3_W)),
            _const_spec((1, GATE_W)),
            _const_spec((1, GATE_W)),
            _const_spec((1, C_HEAD_DIM)),
        ],
        out_specs=pl.BlockSpec((nb, tm, C_W), row),
        out_shape=jax.ShapeDtypeStruct((bsz, t, C_W), BF16),
        scratch_shapes=[
            pltpu.VMEM((nb, 8, C3_W), F32),
            pltpu.VMEM((nb, C_HEADS, C_HEAD_DIM, C_HEAD_DIM), F32),
            pltpu.VMEM((nb * C_HEADS, SB_GDN, C_HEAD_DIM), F32),
        ],
        compiler_params=_params(("parallel", "arbitrary")),
        name="mixer_c_gated_deltanet",
    )(qkv_c, gates, z, conv_w, alog_lane, dtb_lane, gnorm)


def _ffn_kernel(x_ref, oa_ref, ob_ref, oc_ref, mod_ref, wo_ref, g_ref, wup_ref, cw_ref, wd_ref,
                fg_ref, o_ref,
                hb_ref, acc_ref, ug_ref, uu_ref, act_ref, hg_ref, hu_ref, *, final):
    tm = x_ref.shape[1]

    @pl.when(pl.program_id(1) == 0)
    def _():
        hg_ref[...] = jnp.zeros_like(hg_ref)
        hu_ref[...] = jnp.zeros_like(hu_ref)

    gate1 = mod_ref[0, 2:3, :]
    shift2 = mod_ref[0, 3:4, :]
    scale2 = mod_ref[0, 4:5, :]
    gate2 = mod_ref[0, 5:6, :]
    mix = (jnp.dot(oa_ref[0], wo_ref[0:A_W, :], preferred_element_type=F32)
           + jnp.dot(ob_ref[0], wo_ref[A_W:A_W + B_W, :], preferred_element_type=F32)
           + jnp.dot(oc_ref[0], wo_ref[A_W + B_W:, :], preferred_element_type=F32))
    x1 = x_ref[0] + gate1 * mix
    h = _rms(x1, g_ref[...]) * (1.0 + scale2) + shift2
    hb_ref[...] = h.astype(BF16)
    acc_ref[...] = jnp.zeros_like(acc_ref)

    def width(c):
        return min(FF_CH, D_FF - c * FF_CH)

    def gate_cols(c):
        return slice(c * FF_CH, c * FF_CH + width(c))

    def up_cols(c):
        return slice(D_FF + c * FF_CH, D_FF + c * FF_CH + width(c))

    def up_proj(c):
        hb = hb_ref[...]
        w = width(c)
        ug_ref[c % 2, 8:8 + tm, 0:w] = jnp.dot(hb, wup_ref[:, gate_cols(c)], preferred_element_type=F32)
        uu_ref[c % 2, 8:8 + tm, 0:w] = jnp.dot(hb, wup_ref[:, up_cols(c)], preferred_element_type=F32)

    def conv(u_ref, hdr_ref, c, w):
        n = width(c)
        u = u_ref.at[c % 2]
        u[0:8, 0:n] = hdr_ref[c, :, 0:n]
        y = w[0:1, :] * u[6:6 + tm, 0:n] + w[1:2, :] * u[7:7 + tm, 0:n] + w[2:3, :] * u[8:8 + tm, 0:n]
        hdr_ref[c, :, 0:n] = u[tm:tm + 8, 0:n]
        return y

    up_proj(0)
    for c in range(N_FF_CH + 1):
        if c + 1 < N_FF_CH:
            up_proj(c + 1)
        if c >= 1:
            acc_ref[...] += jnp.dot(act_ref[(c - 1) % 2, :, 0:width(c - 1)], wd_ref[gate_cols(c - 1), :],
                                    preferred_element_type=F32)
        if c < N_FF_CH:
            gate = conv(ug_ref, hg_ref, c, cw_ref[:, gate_cols(c)])
            up = conv(uu_ref, hu_ref, c, cw_ref[:, up_cols(c)])
            act_ref[c % 2, :, 0:width(c)] = (_silu(gate) * up).astype(BF16)
    x2 = x1 + gate2 * acc_ref[...]
    if final:
        x2 = _rms(x2, fg_ref[...])
    o_ref[0] = x2


def _out_ffn(x, o_a, o_b, o_c, mod_l, w_out, g, w_up, conv_w, w_down, final_g, final):
    bsz, t, _ = x.shape
    tm = TM_FFN
    row = lambda b, i: (b, i, 0)
    return pl.pallas_call(
        functools.partial(_ffn_kernel, final=final),
        grid=(bsz, t // tm),
        in_specs=[
            pl.BlockSpec((1, tm, D_MODEL), row),
            pl.BlockSpec((1, tm, A_W), row),
            pl.BlockSpec((1, tm, B_W), row),
            pl.BlockSpec((1, tm, C_W), row),
            pl.BlockSpec((1, 6, D_MODEL), lambda b, i: (b, 0, 0)),
            _const_spec((D_MODEL, D_MODEL)),
            _const_spec((1, D_MODEL)),
            _const_spec((D_MODEL, 2 * D_FF)),
            _const_spec((FFN_CONV, 2 * D_FF)),
            _const_spec((D_FF, D_MODEL)),
            _const_spec((1, D_MODEL)),
        ],
        out_specs=pl.BlockSpec((1, tm, D_MODEL), row),
        out_shape=jax.ShapeDtypeStruct((bsz, t, D_MODEL), F32),
        scratch_shapes=[
            pltpu.VMEM((tm, D_MODEL), BF16),
            pltpu.VMEM((tm, D_MODEL), F32),
            pltpu.VMEM((2, tm + 8, FF_CH), F32),
            pltpu.VMEM((2, tm + 8, FF_CH), F32),
            pltpu.VMEM((2, tm, FF_CH), BF16),
            pltpu.VMEM((N_FF_CH, 8, FF_CH), F32),
            pltpu.VMEM((N_FF_CH, 8, FF_CH), F32),
        ],
        compiler_params=_params(("parallel", "arbitrary")),
        name="outproj_convmlp",
    )(x, o_a, o_b, o_c, mod_l, w_out, g, w_up, conv_w, w_down, final_g)


def _lane_vec(values, start):
    return jnp.zeros((1, GATE_W), F32).at[0, start:start + values.shape[0]].set(values)


def kernel(x, c, ada_w, ada_b, norm_mix_g, norm_ffn_g, w_in, w_out, rel_bias, fgate_bias,
           gdn_conv_w, gdn_A_log, gdn_dt_bias, gdn_norm_g, ffn_w_up, ffn_conv_w,
           ffn_w_down, final_norm_g):
    mod = _modulation(c, ada_w, ada_b)
    final_g = final_norm_g.reshape(1, D_MODEL)
    bias_tiles = _rel_bias_tiles(rel_bias)
    for l in range(DEPTH):
        qkv_ab, qkv_c, z, gates = _inproj(
            x, mod[l], norm_mix_g[l].reshape(1, D_MODEL), _arrange_w_in(w_in[l]))
        f_cols = _fgate(gates, _lane_vec(fgate_bias[l], F_LANE))
        o_a = _mixer_a(qkv_ab, bias_tiles[l])
        o_b = _mixer_b(qkv_ab, f_cols)
        o_c = _mixer_c(qkv_c, gates, z, gdn_conv_w[l],
                       _lane_vec(gdn_A_log[l], A_LANE), _lane_vec(gdn_dt_bias[l], A_LANE),
                       gdn_norm_g[l].reshape(1, C_HEAD_DIM))
        x = _out_ffn(
            x, o_a, o_b, o_c, mod[l], w_out[l].astype(BF16),
            norm_ffn_g[l].reshape(1, D_MODEL),
            ffn_w_up[l].astype(BF16), ffn_conv_w[l], ffn_w_down[l].astype(BF16),
            final_g, final=(l == DEPTH - 1))
    return x
```

```python
import functools

import jax
import jax.numpy as jnp
from jax import lax
from jax.experimental import pallas as pl
from jax.experimental.pallas import tpu as pltpu

F32 = jnp.float32
BF16 = jnp.bfloat16

D_MODEL = 1024
DEPTH = 2
CHUNK = 64
EPS = 1e-6
A_HEADS = 4
A_HEAD_DIM = 64
A_LEFT_CHUNKS = 8
REL_PAST = 256
REL_FUTURE = CHUNK - 1
B_HEADS = 4
B_HEAD_DIM = 64
C_HEADS = 4
C_HEAD_DIM = 128
C_CONV = 4
D_FF = 2816
FFN_CONV = 3
A_W = A_HEADS * A_HEAD_DIM
B_W = B_HEADS * B_HEAD_DIM
C_W = C_HEADS * C_HEAD_DIM
AB_W = 3 * A_W + 3 * B_W
C3_W = 3 * C_W
LANES = 128
GATE_W = LANES
F_LANE, BETA_LANE, A_LANE = 0, 4, 8
NEG = -1e30
LOG2E = 1.4426950408889634
F_PARTS = 3
VMEM_LIMIT = 56 * 1024 * 1024

TM_PROJ = 512
TF_GATE = 1024
TQ_A = 512
TQ_B = 512
TM_GDN = 256
SB_GDN = 128
NB_GDN = 2
TM_FFN = 512
FF_CH = 1280
N_FF_CH = -(-D_FF // FF_CH)


def _sigmoid(x):
    return 1.0 / (1.0 + jnp.exp(-x))


def _silu(x):
    return x * _sigmoid(x)


def _softplus(x):
    return jnp.maximum(x, 0.0) + jnp.log1p(jnp.exp(-jnp.abs(x)))


def _log_sigmoid(x):
    return jnp.minimum(x, 0.0) - jnp.log1p(jnp.exp(-jnp.abs(x)))


def _rms(x, g):
    return x * lax.rsqrt(jnp.mean(x * x, axis=-1, keepdims=True) + EPS) * g


def _params(sem):
    return pltpu.CompilerParams(dimension_semantics=sem, vmem_limit_bytes=VMEM_LIMIT)


def _const_spec(shape):
    nd = len(shape)
    return pl.BlockSpec(shape, lambda *_: (0,) * nd, pipeline_mode=pl.Buffered(1))


def _mod_kernel(c_ref, w_ref, b_ref, o_ref):
    c_act = _silu(c_ref[...])
    o_ref[0] = jnp.dot(c_act, w_ref[0], preferred_element_type=F32) + b_ref[0]


def _modulation(c, ada_w, ada_b):
    bsz = c.shape[0]
    rows = 8
    cp = jnp.zeros((rows, D_MODEL), F32).at[:bsz].set(c)
    ncol = 6 * D_MODEL
    cb = 1536
    out = pl.pallas_call(
        _mod_kernel,
        grid=(DEPTH, ncol // cb),
        in_specs=[
            pl.BlockSpec((rows, D_MODEL), lambda l, j: (0, 0)),
            pl.BlockSpec((1, D_MODEL, cb), lambda l, j: (l, 0, j)),
            pl.BlockSpec((1, 1, cb), lambda l, j: (l, 0, j)),
        ],
        out_specs=pl.BlockSpec((1, rows, cb), lambda l, j: (l, 0, j)),
        out_shape=jax.ShapeDtypeStruct((DEPTH, rows, ncol), F32),
        compiler_params=_params(("arbitrary", "arbitrary")),
        name="adaln_modulation",
    )(cp, ada_w, ada_b.reshape(DEPTH, 1, ncol))
    return out[:, :bsz].reshape(DEPTH, bsz, 6, D_MODEL)


_PROJ_W = AB_W + C3_W + C_W + GATE_W


def _inproj_kernel(x_ref, mod_ref, g_ref, w_ref, oab_ref, oc_ref, oz_ref, og_ref):
    x = x_ref[0]
    shift = mod_ref[0, 0:1, :]
    scale = mod_ref[0, 1:2, :]
    h = _rms(x, g_ref[...]) * (1.0 + scale) + shift
    hb = h.astype(BF16)
    cw = 512
    for j in range(AB_W // cw):
        oab_ref[0, :, j * cw:(j + 1) * cw] = jnp.dot(
            hb, w_ref[:, j * cw:(j + 1) * cw], preferred_element_type=F32).astype(BF16)
    for j in range(C3_W // cw):
        oc_ref[0, :, j * cw:(j + 1) * cw] = jnp.dot(
            hb, w_ref[:, AB_W + j * cw:AB_W + (j + 1) * cw], preferred_element_type=F32)
    oz_ref[0] = jnp.dot(hb, w_ref[:, AB_W + C3_W:AB_W + C3_W + C_W],
                        preferred_element_type=F32)
    og_ref[0] = jnp.dot(hb, w_ref[:, AB_W + C3_W + C_W:], preferred_element_type=F32)


def _inproj(x, mod_l, g, w):
    bsz, t, _ = x.shape
    tm = TM_PROJ
    row = lambda b, i: (b, i, 0)
    return pl.pallas_call(
        _inproj_kernel,
        grid=(bsz, t // tm),
        in_specs=[
            pl.BlockSpec((1, tm, D_MODEL), row),
            pl.BlockSpec((1, 6, D_MODEL), lambda b, i: (b, 0, 0)),
            _const_spec((1, D_MODEL)),
            _const_spec((D_MODEL, _PROJ_W)),
        ],
        out_specs=[
            pl.BlockSpec((1, tm, AB_W), row),
            pl.BlockSpec((1, tm, C3_W), row),
            pl.BlockSpec((1, tm, C_W), row),
            pl.BlockSpec((1, tm, GATE_W), row),
        ],
        out_shape=[
            jax.ShapeDtypeStruct((bsz, t, AB_W), BF16),
            jax.ShapeDtypeStruct((bsz, t, C3_W), F32),
            jax.ShapeDtypeStruct((bsz, t, C_W), F32),
            jax.ShapeDtypeStruct((bsz, t, GATE_W), F32),
        ],
        compiler_params=_params(("parallel", "parallel")),
        name="inproj",
    )(x, mod_l, g, w)


def _arrange_w_in(w_in_l):
    a_end = 3 * A_W
    b_end = a_end + 3 * B_W
    f_end = b_end + B_HEADS
    c_end = f_end + C3_W
    beta_end = c_end + C_HEADS
    a_gate_end = beta_end + C_HEADS
    qa_scale = A_HEAD_DIM ** -0.5
    qb_scale = B_HEAD_DIM ** -0.5 * LOG2E
    w_ab = jnp.concatenate([
        w_in_l[:, 0:A_W] * qa_scale, w_in_l[:, A_W:a_end],
        w_in_l[:, a_end:a_end + B_W] * qb_scale, w_in_l[:, a_end + B_W:b_end]], axis=1)
    w_gate = jnp.concatenate([
        w_in_l[:, b_end:f_end], w_in_l[:, c_end:beta_end], w_in_l[:, beta_end:a_gate_end],
        jnp.zeros((D_MODEL, GATE_W - B_HEADS - 2 * C_HEADS), F32)], axis=1)
    w = jnp.concatenate([w_ab, w_in_l[:, f_end:c_end], w_in_l[:, a_gate_end:], w_gate], axis=1)
    return w.astype(BF16)


def _fgate_kernel(g_ref, fb_ref, o_ref, carry_ref):
    @pl.when(pl.program_id(1) == 0)
    def _():
        carry_ref[...] = jnp.zeros_like(carry_ref)

    tf = g_ref.shape[1]
    lane = lax.broadcasted_iota(jnp.int32, (tf, GATE_W), 1)
    row = lax.broadcasted_iota(jnp.int32, (tf, GATE_W), 0)
    cs = jnp.where(lane < B_HEADS, _log_sigmoid(g_ref[0] + fb_ref[...]), 0.0)
    d = 1
    while d < tf:
        cs = cs + jnp.where(row >= d, pltpu.roll(cs, d, axis=0), 0.0)
        d *= 2
    cs = cs + carry_ref[0:1, :]
    carry_ref[...] = jnp.broadcast_to(cs[tf - 1:tf, :], carry_ref.shape)
    rest = cs * (-LOG2E)
    out = jnp.zeros_like(rest)
    for i in range(F_PARTS):
        part = rest.astype(BF16).astype(F32)
        rest = rest - part
        out = out + (part if i == 0 else pltpu.roll(part, i * B_HEADS, axis=1))
    o_ref[0] = out.astype(BF16)


def _fgate(gates, fbias_lane):
    bsz, t, _ = gates.shape
    tf = min(TF_GATE, t)
    return pl.pallas_call(
        _fgate_kernel,
        grid=(bsz, t // tf),
        in_specs=[
            pl.BlockSpec((1, tf, GATE_W), lambda b, i: (b, i, 0)),
            _const_spec((1, GATE_W)),
        ],
        out_specs=pl.BlockSpec((1, tf, GATE_W), lambda b, i: (b, i, 0)),
        out_shape=jax.ShapeDtypeStruct((bsz, t, GATE_W), BF16),
        scratch_shapes=[pltpu.VMEM((8, LANES), F32)],
        compiler_params=_params(("parallel", "arbitrary")),
        name="fgate_cumsum",
    )(gates, fbias_lane)


def _mixa_kernel(q_ref, kp_ref, kc_ref, vp_ref, vc_ref, bias_ref, o_ref):
    i = pl.program_id(1)
    tq = q_ref.shape[1]
    lane = lax.broadcasted_iota(jnp.int32, (1, LANES), 1)
    lo = lane < A_HEAD_DIM
    has_prev = i > 0
    nt = (((1,), (1,)), ((), ()))
    npair = A_HEADS // 2
    logits, values = [], []
    for pair in range(npair):
        sl = slice(pair * LANES, (pair + 1) * LANES)
        q = q_ref[0, :, sl]
        zero = jnp.zeros_like(q)
        q_both = jnp.concatenate([jnp.where(lo, q, zero), jnp.where(lo, zero, q)], axis=0)
        k_both = jnp.concatenate([kp_ref[0, :, sl], kc_ref[0, :, sl]], axis=0)
        values.append(jnp.concatenate([vp_ref[0, :, sl], vc_ref[0, :, sl]], axis=0))
        logits.append(lax.dot_general(q_both, k_both, nt, preferred_element_type=F32))
    for pair in range(npair):
        probs, norms = [], []
        for hh in range(2):
            h = 2 * pair + hh
            s = logits[pair][hh * tq:(hh + 1) * tq]
            s_p = jnp.where(has_prev, s[:, 0:tq] + bias_ref[h, :, 0:tq], NEG)
            s_c = s[:, tq:2 * tq] + bias_ref[h, :, tq:2 * tq]
            m = jnp.maximum(jnp.max(s_p, axis=-1, keepdims=True), jnp.max(s_c, axis=-1, keepdims=True))
            p_p = jnp.exp(s_p - m)
            p_c = jnp.exp(s_c - m)
            norms.append(jnp.sum(p_p, axis=-1, keepdims=True) + jnp.sum(p_c, axis=-1, keepdims=True))
            probs.append(jnp.concatenate([p_p, p_c], axis=1).astype(BF16))
        o = jnp.dot(jnp.concatenate(probs, axis=0), values[pair], preferred_element_type=F32)
        o_ref[0, :, pair * LANES:(pair + 1) * LANES] = jnp.where(
            lo, o[0:tq] / norms[0], o[tq:2 * tq] / norms[1]).astype(BF16)


def _mixer_a(qkv_ab, bias):
    bsz, t, _ = qkv_ab.shape
    tq = TQ_A
    prev = lambda col: (lambda b, i: (b, jnp.maximum(i - 1, 0), col))
    cur = lambda col: (lambda b, i: (b, i, col))
    blk = (1, tq, A_W)
    return pl.pallas_call(
        _mixa_kernel,
        grid=(bsz, t // tq),
        in_specs=[
            pl.BlockSpec(blk, cur(0)),
            pl.BlockSpec(blk, prev(1)), pl.BlockSpec(blk, cur(1)),
            pl.BlockSpec(blk, prev(2)), pl.BlockSpec(blk, cur(2)),
            _const_spec((A_HEADS, tq, 2 * tq)),
        ],
        out_specs=pl.BlockSpec(blk, cur(0)),
        out_shape=jax.ShapeDtypeStruct((bsz, t, A_W), BF16),
        compiler_params=_params(("parallel", "parallel")),
        name="mixer_a_chunk_attn",
    )(qkv_ab, qkv_ab, qkv_ab, qkv_ab, qkv_ab, bias)


_REL_PAD = 384
_REL_ROLL_W = 4 * TQ_A


def _relbias_kernel(tab_ref, o_ref, row_ref):
    tq = o_ref.shape[1]
    m = lax.broadcasted_iota(jnp.int32, (_REL_PAD, _REL_ROLL_W), 1)
    m = jnp.where(m < _REL_ROLL_W // 2, m, m - _REL_ROLL_W)
    idx = jnp.clip(tq - m, -REL_FUTURE, REL_PAST) + REL_FUTURE
    n = lax.broadcasted_iota(jnp.int32, (_REL_PAD, _REL_ROLL_W), 0)
    onehot = (n == idx).astype(F32)
    row_ref[...] = jnp.dot(tab_ref[...], onehot, precision=lax.Precision.HIGHEST,
                           preferred_element_type=F32)
    row = row_ref[pl.ds(pl.program_id(0), 1), :]
    rolled = pltpu.roll(jnp.broadcast_to(row, (tq, _REL_ROLL_W)), 0, 1, stride=1, stride_axis=0)
    qc = lax.broadcasted_iota(jnp.int32, (tq, 2 * tq), 0) // CHUNK
    kc = lax.broadcasted_iota(jnp.int32, (tq, 2 * tq), 1) // CHUNK - tq // CHUNK
    valid = (kc <= qc) & (kc >= qc - A_LEFT_CHUNKS)
    o_ref[0] = jnp.where(valid, rolled[:, 0:2 * tq], NEG)


def _rel_bias_tiles(rel_bias):
    nrel = rel_bias.shape[-1]
    nrow = DEPTH * A_HEADS
    tab = jnp.zeros((nrow, _REL_PAD), F32).at[:, :nrel].set(rel_bias.reshape(nrow, nrel))
    out = pl.pallas_call(
        _relbias_kernel,
        grid=(nrow,),
        in_specs=[pl.BlockSpec((nrow, _REL_PAD), lambda h: (0, 0))],
        out_specs=pl.BlockSpec((1, TQ_A, 2 * TQ_A), lambda h: (h, 0, 0)),
        out_shape=jax.ShapeDtypeStruct((nrow, TQ_A, 2 * TQ_A), F32),
        scratch_shapes=[pltpu.VMEM((nrow, _REL_ROLL_W), F32)],
        compiler_params=_params(("arbitrary",)),
        name="relbias_toeplitz",
    )(tab)
    return out.reshape(DEPTH, A_HEADS, TQ_A, 2 * TQ_A)


def _mixb_kernel(q_ref, k_ref, v_ref, f_ref, o_ref, s_ref, p_ref, m_ref, l_ref, acc_ref):
    pair = pl.program_id(1)
    tq = TQ_B
    tk = tq
    lax.fori_loop(0, q_ref.shape[1] // tq, functools.partial(
        _mixb_query_tile, pair, q_ref, k_ref, v_ref, f_ref, o_ref, s_ref, p_ref, m_ref, l_ref, acc_ref), 0)


def _mixb_query_tile(pair, q_ref, k_ref, v_ref, f_ref, o_ref, s_ref, p_ref, m_ref, l_ref, acc_ref,
                     i, carry):
    tq = TQ_B
    tk = tq
    row0 = pl.multiple_of(i * tq, tq)
    lane = lax.broadcasted_iota(jnp.int32, (1, LANES), 1)
    lo = lane < B_HEAD_DIM
    q = q_ref[0, pl.ds(row0, tq), :]
    zero = jnp.zeros_like(q)
    qs = []
    for hh in range(2):
        head = 2 * pair + hh
        picks = functools.reduce(jnp.logical_or, [lane == head + t * B_HEADS for t in range(F_PARTS)])
        ones = jnp.broadcast_to(jnp.where(picks, 1.0, 0.0).astype(F32), (tq, LANES)).astype(BF16)
        qs.append(jnp.concatenate([jnp.where(lo if hh == 0 else ~lo, q, zero), ones], axis=1))
    nt = (((1,), (1,)), ((), ()))
    m_ref[...] = jnp.full(m_ref.shape, NEG, F32)
    l_ref[...] = jnp.zeros(l_ref.shape, F32)
    acc_ref[...] = jnp.zeros(acc_ref.shape, F32)
    p_ref[...] = jnp.zeros(p_ref.shape, BF16)

    def logits(j):
        off = pl.multiple_of(j * tk, tk)
        kf = jnp.concatenate([k_ref[0, pl.ds(off, tk), :], f_ref[0, pl.ds(off, tk), :]], axis=1)
        return [lax.dot_general(qs[hh], kf, nt, preferred_element_type=F32) for hh in range(2)]

    def pending_pv(j):
        off = pl.multiple_of(jnp.maximum(j - 1, 0) * tk, tk)
        v = v_ref[0, pl.ds(off, tk), :]
        return [jnp.dot(p_ref[hh], v, preferred_element_type=F32) for hh in range(2)]

    def softmax(hh, diagonal):
        s = s_ref[hh]
        if diagonal:
            r = lax.broadcasted_iota(jnp.int32, (tq, tk), 0)
            c = lax.broadcasted_iota(jnp.int32, (tq, tk), 1)
            s = jnp.where(c <= r, s, NEG)
        tiles = [s[:, t * LANES:(t + 1) * LANES] for t in range(tk // LANES)]
        m_tile = functools.reduce(jnp.maximum, tiles)
        m_prev = m_ref[hh]
        m_new = jnp.maximum(m_prev, jnp.max(m_tile, axis=-1, keepdims=True))
        alpha = jnp.exp2(m_prev - m_new)
        p_tiles = [jnp.exp2(t - m_new) for t in tiles]
        l_ref[hh] = alpha * l_ref[hh] + functools.reduce(jnp.add, p_tiles)
        m_ref[hh] = m_new
        return alpha, jnp.concatenate(p_tiles, axis=1).astype(BF16)

    def store(s_pair):
        for hh in range(2):
            s_ref[hh] = s_pair[hh]

    store(logits(0))

    def body(j, carry):
        pv = pending_pv(j)
        nxt = logits(j + 1)
        for hh in range(2):
            alpha, p = softmax(hh, False)
            p_ref[hh] = p
            acc_ref[hh] = alpha * (acc_ref[hh] + pv[hh])
        store(nxt)
        return carry

    lax.fori_loop(0, i, body, 0)
    pv = pending_pv(i)
    v = v_ref[0, pl.ds(pl.multiple_of(i * tk, tk), tk), :]
    outs = []
    for hh in range(2):
        alpha, p = softmax(hh, True)
        acc = alpha * (acc_ref[hh] + pv[hh]) + jnp.dot(p, v, preferred_element_type=F32)
        outs.append(acc / jnp.sum(l_ref[hh], axis=-1, keepdims=True))
    o_ref[0, pl.ds(row0, tq), :] = jnp.where(lo, outs[0], outs[1]).astype(BF16)
    return carry


def _mixer_b(qkv_ab, f_cols):
    bsz, t, _ = qkv_ab.shape
    tq = TQ_B
    base = 3 * A_W // LANES
    npair = B_HEADS // 2
    whole = lambda col: (lambda b, p: (b, 0, col + p))
    return pl.pallas_call(
        _mixb_kernel,
        grid=(bsz, npair),
        in_specs=[
            pl.BlockSpec((1, t, LANES), whole(base)),
            pl.BlockSpec((1, t, LANES), whole(base + npair)),
            pl.BlockSpec((1, t, LANES), whole(base + 2 * npair)),
            pl.BlockSpec((1, t, GATE_W), lambda b, p: (b, 0, 0)),
        ],
        out_specs=pl.BlockSpec((1, t, LANES), whole(0)),
        out_shape=jax.ShapeDtypeStruct((bsz, t, B_W), BF16),
        scratch_shapes=[
            pltpu.VMEM((2, tq, tq), F32),
            pltpu.VMEM((2, tq, tq), BF16),
            pltpu.VMEM((2, tq, LANES), F32),
            pltpu.VMEM((2, tq, LANES), F32),
            pltpu.VMEM((2, tq, LANES), F32),
        ],
        compiler_params=_params(("parallel", "parallel")),
        name="mixer_b_forgetting_attn",
    )(qkv_ab, qkv_ab, qkv_ab, f_cols)


def _gdn_kernel(x_ref, gate_ref, z_ref, cw_ref, alog_ref, dtb_ref, gn_ref, o_ref,
                xs_ref, s_ref, vn_ref):
    nb, tm = x_ref.shape[0], x_ref.shape[1]
    sb = SB_GDN
    nsub = tm // sb
    cps = sb // CHUNK

    @pl.when(pl.program_id(1) == 0)
    def _():
        xs_ref[...] = jnp.zeros_like(xs_ref)
        s_ref[...] = jnp.zeros_like(s_ref)

    ri = lax.broadcasted_iota(jnp.int32, (sb, sb), 0)
    ci = lax.broadcasted_iota(jnp.int32, (sb, sb), 1)
    same = (ri // CHUNK) == (ci // CHUNK)
    tril = same & (ci <= ri)
    strict = same & (ci < ri)
    eye = (ri == ci).astype(F32)

    def level_mask(b):
        return ((ri // (2 * b)) == (ci // (2 * b))) & (((ri // b) % 2) == 1) & (((ci // b) % 2) == 0)

    pos = lax.broadcasted_iota(jnp.int32, (tm, GATE_W), 0) % CHUNK
    row8 = lax.broadcasted_iota(jnp.int32, (8, C3_W), 0)
    sub_chunk = lax.broadcasted_iota(jnp.int32, (sb, 1), 0) // CHUNK
    nt = (((1,), (1,)), ((), ()))
    tn = (((0,), (0,)), ((), ()))

    units = [(bb, sub, h) for bb in range(nb) for sub in range(nsub) for h in range(C_HEADS)]
    gates = {}
    qkvs = {}
    for bb in range(nb):
        xt = x_ref[bb]
        tail = xs_ref[bb]
        y = cw_ref[C_CONV - 1:C_CONV, :] * xt
        for k in range(1, C_CONV):
            head = jnp.where(row8 < k, pltpu.roll(tail, k, axis=0), pltpu.roll(xt[0:8], k, axis=0))
            shifted = jnp.concatenate([head, pltpu.roll(xt, k, axis=0)[8:]], axis=0)
            y = y + cw_ref[C_CONV - 1 - k:C_CONV - k, :] * shifted
        xs_ref[bb] = xt[tm - 8:tm]
        qkvs[bb] = _silu(y)

        gt = gate_ref[bb]
        beta = _sigmoid(gt)
        g = -jnp.exp(alog_ref[...]) * _softplus(gt + dtb_ref[...])
        gc = g
        d = 1
        while d < CHUNK:
            gc = gc + jnp.where(pos >= d, pltpu.roll(gc, d, axis=0), 0.0)
            d *= 2
        gl = jnp.concatenate(
            [jnp.broadcast_to(gc[(c + 1) * CHUNK - 1:(c + 1) * CHUNK, :], (CHUNK, GATE_W))
             for c in range(tm // CHUNK)], axis=0)
        gates[bb] = (beta, gc, gl, gc.T)

    work = {}
    for u in units:
        bb, sub, h = u
        rows = slice(sub * sb, (sub + 1) * sb)
        qkv = qkvs[bb]
        beta, gc, gl, gc_t = gates[bb]
        qh = qkv[rows, h * C_HEAD_DIM:(h + 1) * C_HEAD_DIM]
        kh = qkv[rows, C_W + h * C_HEAD_DIM:C_W + (h + 1) * C_HEAD_DIM]
        vh = qkv[rows, 2 * C_W + h * C_HEAD_DIM:2 * C_W + (h + 1) * C_HEAD_DIM]
        qh = qh * lax.rsqrt(jnp.sum(qh * qh, axis=-1, keepdims=True) + EPS) * (C_HEAD_DIM ** -0.5)
        kh = kh * lax.rsqrt(jnp.sum(kh * kh, axis=-1, keepdims=True) + EPS)
        b_col = beta[rows, BETA_LANE + h:BETA_LANE + h + 1]
        gc_col = gc[rows, A_LANE + h:A_LANE + h + 1]
        gl_col = gl[rows, A_LANE + h:A_LANE + h + 1]
        gc_row = gc_t[A_LANE + h:A_LANE + h + 1, rows]
        kbeta = kh * b_col
        lm = jnp.where(tril, jnp.exp(jnp.where(tril, gc_col - gc_row, 0.0)), 0.0)
        kq = lax.dot_general(jnp.concatenate([kbeta, qh], axis=0).astype(BF16), kh.astype(BF16),
                             nt, preferred_element_type=F32)
        amat = jnp.where(strict, kq[0:sb] * lm, 0.0)
        work[u] = dict(
            amat=amat, attn=jnp.where(tril, kq[sb:2 * sb] * lm, 0.0),
            rhs=jnp.concatenate([vh * b_col, kbeta * jnp.exp(gc_col)], axis=1).astype(BF16),
            qg=qh * jnp.exp(gc_col), kd=kh * jnp.exp(gl_col - gc_col), decay=jnp.exp(gl_col),
            tinv=eye - jnp.where(level_mask(1), amat, 0.0))

    b = 2
    while b < CHUNK:
        mask = level_mask(b)
        tas = {}
        for u in units:
            t16 = work[u]["tinv"].astype(BF16)
            work[u]["t16"] = t16
            tas[u] = jnp.dot(t16, jnp.where(mask, work[u]["amat"], 0.0).astype(BF16),
                             preferred_element_type=F32)
        for u in units:
            work[u]["tinv"] = work[u]["tinv"] - jnp.dot(tas[u].astype(BF16), work[u]["t16"],
                                                        preferred_element_type=F32)
        b *= 2
    for u in units:
        uw = jnp.dot(work[u]["tinv"].astype(BF16), work[u]["rhs"], preferred_element_type=F32)
        work[u]["u"] = uw[:, :C_HEAD_DIM]
        work[u]["w"] = uw[:, C_HEAD_DIM:]

    chains = [(bb, h) for bb in range(nb) for h in range(C_HEADS)]
    outs = {ch: [] for ch in chains}
    for sub in range(nsub):
        for bb, h in chains:
            vn_ref[bb * C_HEADS + h] = work[bb, sub, h]["u"]
        for cc in range(cps):
            rs = slice(cc * CHUNK, (cc + 1) * CHUNK)
            in_chunk = sub_chunk == cc
            first = {}
            for bb, h in chains:
                ph = work[bb, sub, h]
                s_mat = s_ref[bb, h]
                first[bb, h] = (s_mat, jnp.dot(
                    jnp.concatenate([ph["w"][rs], ph["qg"][rs]], axis=0).astype(BF16),
                    s_mat.astype(BF16), preferred_element_type=F32))
            for bb, h in chains:
                ph = work[bb, sub, h]
                s_mat, ws = first[bb, h]
                vn = vn_ref.at[bb * C_HEADS + h]
                vn[rs, :] = ph["u"][rs] - ws[0:CHUNK]
                vn_all = vn[...].astype(BF16)
                outs[bb, h].append(ws[CHUNK:] + jnp.dot(ph["attn"][rs].astype(BF16), vn_all,
                                                        preferred_element_type=F32))
                kd = jnp.where(in_chunk, ph["kd"], 0.0).astype(BF16)
                s_ref[bb, h] = s_mat * ph["decay"][cc * CHUNK:cc * CHUNK + 1] + lax.dot_general(
                    kd, vn_all, tn, preferred_element_type=F32)

    for bb in range(nb):
        for h in range(C_HEADS):
            hs = slice(h * C_HEAD_DIM, (h + 1) * C_HEAD_DIM)
            o = jnp.concatenate(outs[bb, h], axis=0)
            o_ref[bb, :, hs] = (_rms(o, gn_ref[...]) * _silu(z_ref[bb, :, hs])).astype(BF16)


def _mixer_c(qkv_c, gates, z, conv_w, alog_lane, dtb_lane, gnorm):
    bsz, t, _ = qkv_c.shape
    tm = TM_GDN
    nb = NB_GDN if bsz % NB_GDN == 0 else 1
    row = lambda b, i: (b, i, 0)
    return pl.pallas_call(
        _gdn_kernel,
        grid=(bsz // nb, t // tm),
        in_specs=[
            pl.BlockSpec((nb, tm, C3_W), row),
            pl.BlockSpec((nb, tm, GATE_W), row),
            pl.BlockSpec((nb, tm, C_W), row),
            _const_spec((C_CONV, C3_W)),
            _const_spec((1, GATE_W)),
            _const_spec((1, GATE_W)),
            _const_spec((1, C_HEAD_DIM)),
        ],
        out_specs=pl.BlockSpec((nb, tm, C_W), row),
        out_shape=jax.ShapeDtypeStruct((bsz, t, C_W), BF16),
        scratch_shapes=[
            pltpu.VMEM((nb, 8, C3_W), F32),
            pltpu.VMEM((nb, C_HEADS, C_HEAD_DIM, C_HEAD_DIM), F32),
            pltpu.VMEM((nb * C_HEADS, SB_GDN, C_HEAD_DIM), F32),
        ],
        compiler_params=_params(("parallel", "arbitrary")),
        name="mixer_c_gated_deltanet",
    )(qkv_c, gates, z, conv_w, alog_lane, dtb_lane, gnorm)


def _ffn_kernel(x_ref, oa_ref, ob_ref, oc_ref, mod_ref, wo_ref, g_ref, wup_ref, cw_ref, wd_ref,
                fg_ref, o_ref,
                hb_ref, acc_ref, ug_ref, uu_ref, act_ref, hg_ref, hu_ref, *, final):
    tm = x_ref.shape[1]

    @pl.when(pl.program_id(1) == 0)
    def _():
        hg_ref[...] = jnp.zeros_like(hg_ref)
        hu_ref[...] = jnp.zeros_like(hu_ref)

    gate1 = mod_ref[0, 2:3, :]
    shift2 = mod_ref[0, 3:4, :]
    scale2 = mod_ref[0, 4:5, :]
    gate2 = mod_ref[0, 5:6, :]
    mix = (jnp.dot(oa_ref[0], wo_ref[0:A_W, :], preferred_element_type=F32)
           + jnp.dot(ob_ref[0], wo_ref[A_W:A_W + B_W, :], preferred_element_type=F32)
           + jnp.dot(oc_ref[0], wo_ref[A_W + B_W:, :], preferred_element_type=F32))
    x1 = x_ref[0] + gate1 * mix
    h = _rms(x1, g_ref[...]) * (1.0 + scale2) + shift2
    hb_ref[...] = h.astype(BF16)
    acc_ref[...] = jnp.zeros_like(acc_ref)

    def width(c):
        return min(FF_CH, D_FF - c * FF_CH)

    def gate_cols(c):
        return slice(c * FF_CH, c * FF_CH + width(c))

    def up_cols(c):
        return slice(D_FF + c * FF_CH, D_FF + c * FF_CH + width(c))

    def up_proj(c):
        hb = hb_ref[...]
        w = width(c)
        ug_ref[c % 2, 8:8 + tm, 0:w] = jnp.dot(hb, wup_ref[:, gate_cols(c)], preferred_element_type=F32)
        uu_ref[c % 2, 8:8 + tm, 0:w] = jnp.dot(hb, wup_ref[:, up_cols(c)], preferred_element_type=F32)

    def conv(u_ref, hdr_ref, c, w):
        n = width(c)
        u = u_ref.at[c % 2]
        u[0:8, 0:n] = hdr_ref[c, :, 0:n]
        y = w[0:1, :] * u[6:6 + tm, 0:n] + w[1:2, :] * u[7:7 + tm, 0:n] + w[2:3, :] * u[8:8 + tm, 0:n]
        hdr_ref[c, :, 0:n] = u[tm:tm + 8, 0:n]
        return y

    up_proj(0)
    for c in range(N_FF_CH + 1):
        if c + 1 < N_FF_CH:
            up_proj(c + 1)
        if c >= 1:
            acc_ref[...] += jnp.dot(act_ref[(c - 1) % 2, :, 0:width(c - 1)], wd_ref[gate_cols(c - 1), :],
                                    preferred_element_type=F32)
        if c < N_FF_CH:
            gate = conv(ug_ref, hg_ref, c, cw_ref[:, gate_cols(c)])
            up = conv(uu_ref, hu_ref, c, cw_ref[:, up_cols(c)])
            act_ref[c % 2, :, 0:width(c)] = (_silu(gate) * up).astype(BF16)
    x2 = x1 + gate2 * acc_ref[...]
    if final:
        x2 = _rms(x2, fg_ref[...])
    o_ref[0] = x2


def _out_ffn(x, o_a, o_b, o_c, mod_l, w_out, g, w_up, conv_w, w_down, final_g, final):
    bsz, t, _ = x.shape
    tm = TM_FFN
    row = lambda b, i: (b, i, 0)
    return pl.pallas_call(
        functools.partial(_ffn_kernel, final=final),
        grid=(bsz, t // tm),
        in_specs=[
            pl.BlockSpec((1, tm, D_MODEL), row),
            pl.BlockSpec((1, tm, A_W), row),
            pl.BlockSpec((1, tm, B_W), row),
            pl.BlockSpec((1, tm, C_W), row),
            pl.BlockSpec((1, 6, D_MODEL), lambda b, i: (b, 0, 0)),
            _const_spec((D_MODEL, D_MODEL)),
            _const_spec((1, D_MODEL)),
            _const_spec((D_MODEL, 2 * D_FF)),
            _const_spec((FFN_CONV, 2 * D_FF)),
            _const_spec((D_FF, D_MODEL)),
            _const_spec((1, D_MODEL)),
        ],
        out_specs=pl.BlockSpec((1, tm, D_MODEL), row),
        out_shape=jax.ShapeDtypeStruct((bsz, t, D_MODEL), F32),
        scratch_shapes=[
            pltpu.VMEM((tm, D_MODEL), BF16),
            pltpu.VMEM((tm, D_MODEL), F32),
            pltpu.VMEM((2, tm + 8, FF_CH), F32),
            pltpu.VMEM((2, tm + 8, FF_CH), F32),
            pltpu.VMEM((2, tm, FF_CH), BF16),
            pltpu.VMEM((N_FF_CH, 8, FF_CH), F32),
            pltpu.VMEM((N_FF_CH, 8, FF_CH), F32),
        ],
        compiler_params=_params(("parallel", "arbitrary")),
        name="outproj_convmlp",
    )(x, o_a, o_b, o_c, mod_l, w_out, g, w_up, conv_w, w_down, final_g)


def _lane_vec(values, start):
    return jnp.zeros((1, GATE_W), F32).at[0, start:start + values.shape[0]].set(values)


def kernel(x, c, ada_w, ada_b, norm_mix_g, norm_ffn_g, w_in, w_out, rel_bias, fgate_bias,
           gdn_conv_w, gdn_A_log, gdn_dt_bias, gdn_norm_g, ffn_w_up, ffn_conv_w,
           ffn_w_down, final_norm_g):
    mod = _modulation(c, ada_w, ada_b)
    final_g = final_norm_g.reshape(1, D_MODEL)
    bias_tiles = _rel_bias_tiles(rel_bias)
    for l in range(DEPTH):
        qkv_ab, qkv_c, z, gates = _inproj(
            x, mod[l], norm_mix_g[l].reshape(1, D_MODEL), _arrange_w_in(w_in[l]))
        f_cols = _fgate(gates, _lane_vec(fgate_bias[l], F_LANE))
        o_a = _mixer_a(qkv_ab, bias_tiles[l])
        o_b = _mixer_b(qkv_ab, f_cols)
        o_c = _mixer_c(qkv_c, gates, z, gdn_conv_w[l],
                       _lane_vec(gdn_A_log[l], A_LANE), _lane_vec(gdn_dt_bias[l], A_LANE),
                       gdn_norm_g[l].reshape(1, C_HEAD_DIM))
        x = _out_ffn(
            x, o_a, o_b, o_c, mod[l], w_out[l].astype(BF16),
            norm_ffn_g[l].reshape(1, D_MODEL),
            ffn_w_up[l].astype(BF16), ffn_conv_w[l], ffn_w_down[l].astype(BF16),
            final_g, final=(l == DEPTH - 1))
    return x
```

```python
import functools

import jax
import jax.numpy as jnp
from jax import lax
from jax.experimental import pallas as pl
from jax.experimental.pallas import tpu as pltpu

F32 = jnp.float32
BF16 = jnp.bfloat16

D_MODEL = 1024
DEPTH = 2
CHUNK = 64
EPS = 1e-6
A_HEADS = 4
A_HEAD_DIM = 64
A_LEFT_CHUNKS = 8
REL_PAST = 256
REL_FUTURE = CHUNK - 1
B_HEADS = 4
B_HEAD_DIM = 64
C_HEADS = 4
C_HEAD_DIM = 128
C_CONV = 4
D_FF = 2816
FFN_CONV = 3
A_W = A_HEADS * A_HEAD_DIM
B_W = B_HEADS * B_HEAD_DIM
C_W = C_HEADS * C_HEAD_DIM
AB_W = 3 * A_W + 3 * B_W
C3_W = 3 * C_W
LANES = 128
GATE_W = LANES
F_LANE, BETA_LANE, A_LANE = 0, 4, 8
NEG = -1e30
LOG2E = 1.4426950408889634
F_PARTS = 3
VMEM_LIMIT = 56 * 1024 * 1024

TM_PROJ = 512
TF_GATE = 1024
TQ_A = 512
TQ_B = 512
TM_GDN = 256
SB_GDN = 128
NB_GDN = 2
TM_FFN = 512
FF_CH = 1280
N_FF_CH = -(-D_FF // FF_CH)


def _sigmoid(x):
    return 1.0 / (1.0 + jnp.exp(-x))


def _silu(x):
    return x * _sigmoid(x)


def _softplus(x):
    return jnp.maximum(x, 0.0) + jnp.log1p(jnp.exp(-jnp.abs(x)))


def _log_sigmoid(x):
    return jnp.minimum(x, 0.0) - jnp.log1p(jnp.exp(-jnp.abs(x)))


def _rms(x, g):
    return x * lax.rsqrt(jnp.mean(x * x, axis=-1, keepdims=True) + EPS) * g


def _params(sem):
    return pltpu.CompilerParams(dimension_semantics=sem, vmem_limit_bytes=VMEM_LIMIT)


def _const_spec(shape):
    nd = len(shape)
    return pl.BlockSpec(shape, lambda *_: (0,) * nd, pipeline_mode=pl.Buffered(1))


def _mod_kernel(c_ref, w_ref, b_ref, o_ref):
    c_act = _silu(c_ref[...])
    o_ref[0] = jnp.dot(c_act, w_ref[0], preferred_element_type=F32) + b_ref[0]


def _modulation(c, ada_w, ada_b):
    bsz = c.shape[0]
    rows = 8
    cp = jnp.zeros((rows, D_MODEL), F32).at[:bsz].set(c)
    ncol = 6 * D_MODEL
    cb = 1536
    out = pl.pallas_call(
        _mod_kernel,
        grid=(DEPTH, ncol // cb),
        in_specs=[
            pl.BlockSpec((rows, D_MODEL), lambda l, j: (0, 0)),
            pl.BlockSpec((1, D_MODEL, cb), lambda l, j: (l, 0, j)),
            pl.BlockSpec((1, 1, cb), lambda l, j: (l, 0, j)),
        ],
        out_specs=pl.BlockSpec((1, rows, cb), lambda l, j: (l, 0, j)),
        out_shape=jax.ShapeDtypeStruct((DEPTH, rows, ncol), F32),
        compiler_params=_params(("arbitrary", "arbitrary")),
        name="adaln_modulation",
    )(cp, ada_w, ada_b.reshape(DEPTH, 1, ncol))
    return out[:, :bsz].reshape(DEPTH, bsz, 6, D_MODEL)


_PROJ_W = AB_W + C3_W + C_W + GATE_W


def _inproj_kernel(x_ref, mod_ref, g_ref, w_ref, oab_ref, oc_ref, oz_ref, og_ref):
    x = x_ref[0]
    shift = mod_ref[0, 0:1, :]
    scale = mod_ref[0, 1:2, :]
    h = _rms(x, g_ref[...]) * (1.0 + scale) + shift
    hb = h.astype(BF16)
    cw = 512
    for j in range(AB_W // cw):
        oab_ref[0, :, j * cw:(j + 1) * cw] = jnp.dot(
            hb, w_ref[:, j * cw:(j + 1) * cw], preferred_element_type=F32).astype(BF16)
    for j in range(C3_W // cw):
        oc_ref[0, :, j * cw:(j + 1) * cw] = jnp.dot(
            hb, w_ref[:, AB_W + j * cw:AB_W + (j + 1) * cw], preferred_element_type=F32)
    oz_ref[0] = jnp.dot(hb, w_ref[:, AB_W + C3_W:AB_W + C3_W + C_W],
                        preferred_element_type=F32)
    og_ref[0] = jnp.dot(hb, w_ref[:, AB_W + C3_W + C_W:], preferred_element_type=F32)


def _inproj(x, mod_l, g, w):
    bsz, t, _ = x.shape
    tm = TM_PROJ
    row = lambda b, i: (b, i, 0)
    return pl.pallas_call(
        _inproj_kernel,
        grid=(bsz, t // tm),
        in_specs=[
            pl.BlockSpec((1, tm, D_MODEL), row),
            pl.BlockSpec((1, 6, D_MODEL), lambda b, i: (b, 0, 0)),
            _const_spec((1, D_MODEL)),
            _const_spec((D_MODEL, _PROJ_W)),
        ],
        out_specs=[
            pl.BlockSpec((1, tm, AB_W), row),
            pl.BlockSpec((1, tm, C3_W), row),
            pl.BlockSpec((1, tm, C_W), row),
            pl.BlockSpec((1, tm, GATE_W), row),
        ],
        out_shape=[
            jax.ShapeDtypeStruct((bsz, t, AB_W), BF16),
            jax.ShapeDtypeStruct((bsz, t, C3_W), F32),
            jax.ShapeDtypeStruct((bsz, t, C_W), F32),
            jax.ShapeDtypeStruct((bsz, t, GATE_W), F32),
        ],
        compiler_params=_params(("parallel", "parallel")),
        name="inproj",
    )(x, mod_l, g, w)


def _arrange_w_in(w_in_l):
    a_end = 3 * A_W
    b_end = a_end + 3 * B_W
    f_end = b_end + B_HEADS
    c_end = f_end + C3_W
    beta_end = c_end + C_HEADS
    a_gate_end = beta_end + C_HEADS
    qa_scale = A_HEAD_DIM ** -0.5
    qb_scale = B_HEAD_DIM ** -0.5 * LOG2E
    w_ab = jnp.concatenate([
        w_in_l[:, 0:A_W] * qa_scale, w_in_l[:, A_W:a_end],
        w_in_l[:, a_end:a_end + B_W] * qb_scale, w_in_l[:, a_end + B_W:b_end]], axis=1)
    w_gate = jnp.concatenate([
        w_in_l[:, b_end:f_end], w_in_l[:, c_end:beta_end], w_in_l[:, beta_end:a_gate_end],
        jnp.zeros((D_MODEL, GATE_W - B_HEADS - 2 * C_HEADS), F32)], axis=1)
    w = jnp.concatenate([w_ab, w_in_l[:, f_end:c_end], w_in_l[:, a_gate_end:], w_gate], axis=1)
    return w.astype(BF16)


def _fgate_kernel(g_ref, fb_ref, o_ref, carry_ref):
    @pl.when(pl.program_id(1) == 0)
    def _():
        carry_ref[...] = jnp.zeros_like(carry_ref)

    tf = g_ref.shape[1]
    lane = lax.broadcasted_iota(jnp.int32, (tf, GATE_W), 1)
    row = lax.broadcasted_iota(jnp.int32, (tf, GATE_W), 0)
    cs = jnp.where(lane < B_HEADS, _log_sigmoid(g_ref[0] + fb_ref[...]), 0.0)
    d = 1
    while d < tf:
        cs = cs + jnp.where(row >= d, pltpu.roll(cs, d, axis=0), 0.0)
        d *= 2
    cs = cs + carry_ref[0:1, :]
    carry_ref[...] = jnp.broadcast_to(cs[tf - 1:tf, :], carry_ref.shape)
    rest = cs * (-LOG2E)
    out = jnp.zeros_like(rest)
    for i in range(F_PARTS):
        part = rest.astype(BF16).astype(F32)
        rest = rest - part
        out = out + (part if i == 0 else pltpu.roll(part, i * B_HEADS, axis=1))
    o_ref[0] = out.astype(BF16)


def _fgate(gates, fbias_lane):
    bsz, t, _ = gates.shape
    tf = min(TF_GATE, t)
    return pl.pallas_call(
        _fgate_kernel,
        grid=(bsz, t // tf),
        in_specs=[
            pl.BlockSpec((1, tf, GATE_W), lambda b, i: (b, i, 0)),
            _const_spec((1, GATE_W)),
        ],
        out_specs=pl.BlockSpec((1, tf, GATE_W), lambda b, i: (b, i, 0)),
        out_shape=jax.ShapeDtypeStruct((bsz, t, GATE_W), BF16),
        scratch_shapes=[pltpu.VMEM((8, LANES), F32)],
        compiler_params=_params(("parallel", "arbitrary")),
        name="fgate_cumsum",
    )(gates, fbias_lane)


def _mixa_kernel(q_ref, kp_ref, kc_ref, vp_ref, vc_ref, bias_ref, o_ref):
    i = pl.program_id(1)
    tq = q_ref.shape[1]
    lane = lax.broadcasted_iota(jnp.int32, (1, LANES), 1)
    lo = lane < A_HEAD_DIM
    has_prev = i > 0
    nt = (((1,), (1,)), ((), ()))
    npair = A_HEADS // 2
    logits, values = [], []
    for pair in range(npair):
        sl = slice(pair * LANES, (pair + 1) * LANES)
        q = q_ref[0, :, sl]
        zero = jnp.zeros_like(q)
        q_both = jnp.concatenate([jnp.where(lo, q, zero), jnp.where(lo, zero, q)], axis=0)
        k_both = jnp.concatenate([kp_ref[0, :, sl], kc_ref[0, :, sl]], axis=0)
        values.append(jnp.concatenate([vp_ref[0, :, sl], vc_ref[0, :, sl]], axis=0))
        logits.append(lax.dot_general(q_both, k_both, nt, preferred_element_type=F32))
    for pair in range(npair):
        probs, norms = [], []
        for hh in range(2):
            h = 2 * pair + hh
            s = logits[pair][hh * tq:(hh + 1) * tq]
            s_p = jnp.where(has_prev, s[:, 0:tq] + bias_ref[h, :, 0:tq], NEG)
            s_c = s[:, tq:2 * tq] + bias_ref[h, :, tq:2 * tq]
            m = jnp.maximum(jnp.max(s_p, axis=-1, keepdims=True), jnp.max(s_c, axis=-1, keepdims=True))
            p_p = jnp.exp(s_p - m)
            p_c = jnp.exp(s_c - m)
            norms.append(jnp.sum(p_p, axis=-1, keepdims=True) + jnp.sum(p_c, axis=-1, keepdims=True))
            probs.append(jnp.concatenate([p_p, p_c], axis=1).astype(BF16))
        o = jnp.dot(jnp.concatenate(probs, axis=0), values[pair], preferred_element_type=F32)
        o_ref[0, :, pair * LANES:(pair + 1) * LANES] = jnp.where(
            lo, o[0:tq] / norms[0], o[tq:2 * tq] / norms[1]).astype(BF16)


def _mixer_a(qkv_ab, bias):
    bsz, t, _ = qkv_ab.shape
    tq = TQ_A
    prev = lambda col: (lambda b, i: (b, jnp.maximum(i - 1, 0), col))
    cur = lambda col: (lambda b, i: (b, i, col))
    blk = (1, tq, A_W)
    return pl.pallas_call(
        _mixa_kernel,
        grid=(bsz, t // tq),
        in_specs=[
            pl.BlockSpec(blk, cur(0)),
            pl.BlockSpec(blk, prev(1)), pl.BlockSpec(blk, cur(1)),
            pl.BlockSpec(blk, prev(2)), pl.BlockSpec(blk, cur(2)),
            _const_spec((A_HEADS, tq, 2 * tq)),
        ],
        out_specs=pl.BlockSpec(blk, cur(0)),
        out_shape=jax.ShapeDtypeStruct((bsz, t, A_W), BF16),
        compiler_params=_params(("parallel", "parallel")),
        name="mixer_a_chunk_attn",
    )(qkv_ab, qkv_ab, qkv_ab, qkv_ab, qkv_ab, bias)


_REL_PAD = 384
_REL_ROLL_W = 4 * TQ_A


def _relbias_kernel(tab_ref, o_ref, row_ref):
    tq = o_ref.shape[1]
    m = lax.broadcasted_iota(jnp.int32, (_REL_PAD, _REL_ROLL_W), 1)
    m = jnp.where(m < _REL_ROLL_W // 2, m, m - _REL_ROLL_W)
    idx = jnp.clip(tq - m, -REL_FUTURE, REL_PAST) + REL_FUTURE
    n = lax.broadcasted_iota(jnp.int32, (_REL_PAD, _REL_ROLL_W), 0)
    onehot = (n == idx).astype(F32)
    row_ref[...] = jnp.dot(tab_ref[...], onehot, precision=lax.Precision.HIGHEST,
                           preferred_element_type=F32)
    row = row_ref[pl.ds(pl.program_id(0), 1), :]
    rolled = pltpu.roll(jnp.broadcast_to(row, (tq, _REL_ROLL_W)), 0, 1, stride=1, stride_axis=0)
    qc = lax.broadcasted_iota(jnp.int32, (tq, 2 * tq), 0) // CHUNK
    kc = lax.broadcasted_iota(jnp.int32, (tq, 2 * tq), 1) // CHUNK - tq // CHUNK
    valid = (kc <= qc) & (kc >= qc - A_LEFT_CHUNKS)
    o_ref[0] = jnp.where(valid, rolled[:, 0:2 * tq], NEG)


def _rel_bias_tiles(rel_bias):
    nrel = rel_bias.shape[-1]
    nrow = DEPTH * A_HEADS
    tab = jnp.zeros((nrow, _REL_PAD), F32).at[:, :nrel].set(rel_bias.reshape(nrow, nrel))
    out = pl.pallas_call(
        _relbias_kernel,
        grid=(nrow,),
        in_specs=[pl.BlockSpec((nrow, _REL_PAD), lambda h: (0, 0))],
        out_specs=pl.BlockSpec((1, TQ_A, 2 * TQ_A), lambda h: (h, 0, 0)),
        out_shape=jax.ShapeDtypeStruct((nrow, TQ_A, 2 * TQ_A), F32),
        scratch_shapes=[pltpu.VMEM((nrow, _REL_ROLL_W), F32)],
        compiler_params=_params(("arbitrary",)),
        name="relbias_toeplitz",
    )(tab)
    return out.reshape(DEPTH, A_HEADS, TQ_A, 2 * TQ_A)


def _mixb_kernel(q_ref, k_ref, v_ref, f_ref, o_ref, s_ref, p_ref, m_ref, l_ref, acc_ref):
    pair = pl.program_id(1)
    tq = TQ_B
    tk = tq
    lax.fori_loop(0, q_ref.shape[1] // tq, functools.partial(
        _mixb_query_tile, pair, q_ref, k_ref, v_ref, f_ref, o_ref, s_ref, p_ref, m_ref, l_ref, acc_ref), 0)


def _mixb_query_tile(pair, q_ref, k_ref, v_ref, f_ref, o_ref, s_ref, p_ref, m_ref, l_ref, acc_ref,
                     i, carry):
    tq = TQ_B
    tk = tq
    row0 = pl.multiple_of(i * tq, tq)
    lane = lax.broadcasted_iota(jnp.int32, (1, LANES), 1)
    lo = lane < B_HEAD_DIM
    q = q_ref[0, pl.ds(row0, tq), :]
    zero = jnp.zeros_like(q)
    qs = []
    for hh in range(2):
        head = 2 * pair + hh
        picks = functools.reduce(jnp.logical_or, [lane == head + t * B_HEADS for t in range(F_PARTS)])
        ones = jnp.broadcast_to(jnp.where(picks, 1.0, 0.0).astype(F32), (tq, LANES)).astype(BF16)
        qs.append(jnp.concatenate([jnp.where(lo if hh == 0 else ~lo, q, zero), ones], axis=1))
    nt = (((1,), (1,)), ((), ()))
    m_ref[...] = jnp.full(m_ref.shape, NEG, F32)
    l_ref[...] = jnp.zeros(l_ref.shape, F32)
    acc_ref[...] = jnp.zeros(acc_ref.shape, F32)
    p_ref[...] = jnp.zeros(p_ref.shape, BF16)

    q_both = jnp.concatenate(qs, axis=0)

    def logits(j):
        off = pl.multiple_of(j * tk, tk)
        kf = jnp.concatenate([k_ref[0, pl.ds(off, tk), :], f_ref[0, pl.ds(off, tk), :]], axis=1)
        return lax.dot_general(q_both, kf, nt, preferred_element_type=F32).reshape(2, tq, tk)

    def pending_pv(j):
        off = pl.multiple_of(jnp.maximum(j - 1, 0) * tk, tk)
        v = v_ref[0, pl.ds(off, tk), :]
        out = jnp.dot(p_ref[...].reshape(2 * tq, tk), v, preferred_element_type=F32)
        return [out[0:tq], out[tq:2 * tq]]

    def softmax(hh, diagonal):
        s = s_ref[hh]
        if diagonal:
            r = lax.broadcasted_iota(jnp.int32, (tq, tk), 0)
            c = lax.broadcasted_iota(jnp.int32, (tq, tk), 1)
            s = jnp.where(c <= r, s, NEG)
        tiles = [s[:, t * LANES:(t + 1) * LANES] for t in range(tk // LANES)]
        m_tile = functools.reduce(jnp.maximum, tiles)
        m_prev = m_ref[hh]
        m_new = jnp.maximum(m_prev, jnp.max(m_tile, axis=-1, keepdims=True))
        alpha = jnp.exp2(m_prev - m_new)
        p_tiles = [jnp.exp2(t - m_new) for t in tiles]
        l_ref[hh] = alpha * l_ref[hh] + functools.reduce(jnp.add, p_tiles)
        m_ref[hh] = m_new
        return alpha, jnp.concatenate(p_tiles, axis=1).astype(BF16)

    s_ref[...] = logits(0)

    def body(j, carry):
        pv = pending_pv(j)
        nxt = logits(j + 1)
        for hh in range(2):
            alpha, p = softmax(hh, False)
            p_ref[hh] = p
            acc_ref[hh] = alpha * (acc_ref[hh] + pv[hh])
        s_ref[...] = nxt
        return carry

    lax.fori_loop(0, i, body, 0)
    pv = pending_pv(i)
    v = v_ref[0, pl.ds(pl.multiple_of(i * tk, tk), tk), :]
    outs = []
    for hh in range(2):
        alpha, p = softmax(hh, True)
        acc = alpha * (acc_ref[hh] + pv[hh]) + jnp.dot(p, v, preferred_element_type=F32)
        outs.append(acc / jnp.sum(l_ref[hh], axis=-1, keepdims=True))
    o_ref[0, pl.ds(row0, tq), :] = jnp.where(lo, outs[0], outs[1]).astype(BF16)
    return carry


def _mixer_b(qkv_ab, f_cols):
    bsz, t, _ = qkv_ab.shape
    tq = TQ_B
    base = 3 * A_W // LANES
    npair = B_HEADS // 2
    whole = lambda col: (lambda b, p: (b, 0, col + p))
    return pl.pallas_call(
        _mixb_kernel,
        grid=(bsz, npair),
        in_specs=[
            pl.BlockSpec((1, t, LANES), whole(base)),
            pl.BlockSpec((1, t, LANES), whole(base + npair)),
            pl.BlockSpec((1, t, LANES), whole(base + 2 * npair)),
            pl.BlockSpec((1, t, GATE_W), lambda b, p: (b, 0, 0)),
        ],
        out_specs=pl.BlockSpec((1, t, LANES), whole(0)),
        out_shape=jax.ShapeDtypeStruct((bsz, t, B_W), BF16),
        scratch_shapes=[
            pltpu.VMEM((2, tq, tq), F32),
            pltpu.VMEM((2, tq, tq), BF16),
            pltpu.VMEM((2, tq, LANES), F32),
            pltpu.VMEM((2, tq, LANES), F32),
            pltpu.VMEM((2, tq, LANES), F32),
        ],
        compiler_params=_params(("parallel", "parallel")),
        name="mixer_b_forgetting_attn",
    )(qkv_ab, qkv_ab, qkv_ab, f_cols)


def _gdn_kernel(x_ref, gate_ref, z_ref, cw_ref, alog_ref, dtb_ref, gn_ref, o_ref,
                xs_ref, s_ref, vn_ref):
    nb, tm = x_ref.shape[0], x_ref.shape[1]
    sb = SB_GDN
    nsub = tm // sb
    cps = sb // CHUNK

    @pl.when(pl.program_id(1) == 0)
    def _():
        xs_ref[...] = jnp.zeros_like(xs_ref)
        s_ref[...] = jnp.zeros_like(s_ref)

    ri = lax.broadcasted_iota(jnp.int32, (sb, sb), 0)
    ci = lax.broadcasted_iota(jnp.int32, (sb, sb), 1)
    same = (ri // CHUNK) == (ci // CHUNK)
    tril = same & (ci <= ri)
    strict = same & (ci < ri)
    eye = (ri == ci).astype(F32)

    def level_mask(b):
        return ((ri // (2 * b)) == (ci // (2 * b))) & (((ri // b) % 2) == 1) & (((ci // b) % 2) == 0)

    pos = lax.broadcasted_iota(jnp.int32, (tm, GATE_W), 0) % CHUNK
    row8 = lax.broadcasted_iota(jnp.int32, (8, C3_W), 0)
    sub_chunk = lax.broadcasted_iota(jnp.int32, (sb, 1), 0) // CHUNK
    nt = (((1,), (1,)), ((), ()))
    tn = (((0,), (0,)), ((), ()))

    units = [(bb, sub, h) for bb in range(nb) for sub in range(nsub) for h in range(C_HEADS)]
    gates = {}
    qkvs = {}
    for bb in range(nb):
        xt = x_ref[bb]
        tail = xs_ref[bb]
        y = cw_ref[C_CONV - 1:C_CONV, :] * xt
        for k in range(1, C_CONV):
            head = jnp.where(row8 < k, pltpu.roll(tail, k, axis=0), pltpu.roll(xt[0:8], k, axis=0))
            shifted = jnp.concatenate([head, pltpu.roll(xt, k, axis=0)[8:]], axis=0)
            y = y + cw_ref[C_CONV - 1 - k:C_CONV - k, :] * shifted
        xs_ref[bb] = xt[tm - 8:tm]
        qkvs[bb] = _silu(y)

        gt = gate_ref[bb]
        beta = _sigmoid(gt)
        g = -jnp.exp(alog_ref[...]) * _softplus(gt + dtb_ref[...])
        gc = g
        d = 1
        while d < CHUNK:
            gc = gc + jnp.where(pos >= d, pltpu.roll(gc, d, axis=0), 0.0)
            d *= 2
        gl = jnp.concatenate(
            [jnp.broadcast_to(gc[(c + 1) * CHUNK - 1:(c + 1) * CHUNK, :], (CHUNK, GATE_W))
             for c in range(tm // CHUNK)], axis=0)
        gates[bb] = (beta, gc, gl, gc.T)

    work = {}
    for u in units:
        bb, sub, h = u
        rows = slice(sub * sb, (sub + 1) * sb)
        qkv = qkvs[bb]
        beta, gc, gl, gc_t = gates[bb]
        qh = qkv[rows, h * C_HEAD_DIM:(h + 1) * C_HEAD_DIM]
        kh = qkv[rows, C_W + h * C_HEAD_DIM:C_W + (h + 1) * C_HEAD_DIM]
        vh = qkv[rows, 2 * C_W + h * C_HEAD_DIM:2 * C_W + (h + 1) * C_HEAD_DIM]
        qh = qh * lax.rsqrt(jnp.sum(qh * qh, axis=-1, keepdims=True) + EPS) * (C_HEAD_DIM ** -0.5)
        kh = kh * lax.rsqrt(jnp.sum(kh * kh, axis=-1, keepdims=True) + EPS)
        b_col = beta[rows, BETA_LANE + h:BETA_LANE + h + 1]
        gc_col = gc[rows, A_LANE + h:A_LANE + h + 1]
        gl_col = gl[rows, A_LANE + h:A_LANE + h + 1]
        gc_row = gc_t[A_LANE + h:A_LANE + h + 1, rows]
        kbeta = kh * b_col
        lm = jnp.where(tril, jnp.exp(jnp.where(tril, gc_col - gc_row, 0.0)), 0.0)
        kq = lax.dot_general(jnp.concatenate([kbeta, qh], axis=0).astype(BF16), kh.astype(BF16),
                             nt, preferred_element_type=F32)
        amat = jnp.where(strict, kq[0:sb] * lm, 0.0)
        work[u] = dict(
            amat=amat, attn=jnp.where(tril, kq[sb:2 * sb] * lm, 0.0),
            rhs=jnp.concatenate([vh * b_col, kbeta * jnp.exp(gc_col)], axis=1).astype(BF16),
            qg=qh * jnp.exp(gc_col), kd=kh * jnp.exp(gl_col - gc_col), decay=jnp.exp(gl_col),
            tinv=eye - jnp.where(level_mask(1), amat, 0.0))

    b = 2
    while b < CHUNK:
        mask = level_mask(b)
        tas = {}
        for u in units:
            t16 = work[u]["tinv"].astype(BF16)
            work[u]["t16"] = t16
            tas[u] = jnp.dot(t16, jnp.where(mask, work[u]["amat"], 0.0).astype(BF16),
                             preferred_element_type=F32)
        for u in units:
            work[u]["tinv"] = work[u]["tinv"] - jnp.dot(tas[u].astype(BF16), work[u]["t16"],
                                                        preferred_element_type=F32)
        b *= 2
    for u in units:
        uw = jnp.dot(work[u]["tinv"].astype(BF16), work[u]["rhs"], preferred_element_type=F32)
        work[u]["u"] = uw[:, :C_HEAD_DIM]
        work[u]["w"] = uw[:, C_HEAD_DIM:]

    chains = [(bb, h) for bb in range(nb) for h in range(C_HEADS)]
    outs = {ch: [] for ch in chains}
    for sub in range(nsub):
        for bb, h in chains:
            vn_ref[bb * C_HEADS + h] = work[bb, sub, h]["u"]
        for cc in range(cps):
            rs = slice(cc * CHUNK, (cc + 1) * CHUNK)
            in_chunk = sub_chunk == cc
            first = {}
            for bb, h in chains:
                ph = work[bb, sub, h]
                s_mat = s_ref[bb, h]
                first[bb, h] = (s_mat, jnp.dot(
                    jnp.concatenate([ph["w"][rs], ph["qg"][rs]], axis=0).astype(BF16),
                    s_mat.astype(BF16), preferred_element_type=F32))
            for bb, h in chains:
                ph = work[bb, sub, h]
                s_mat, ws = first[bb, h]
                vn = vn_ref.at[bb * C_HEADS + h]
                vn[rs, :] = ph["u"][rs] - ws[0:CHUNK]
                vn_all = vn[...].astype(BF16)
                outs[bb, h].append(ws[CHUNK:] + jnp.dot(ph["attn"][rs].astype(BF16), vn_all,
                                                        preferred_element_type=F32))
                kd = jnp.where(in_chunk, ph["kd"], 0.0).astype(BF16)
                s_ref[bb, h] = s_mat * ph["decay"][cc * CHUNK:cc * CHUNK + 1] + lax.dot_general(
                    kd, vn_all, tn, preferred_element_type=F32)

    for bb in range(nb):
        for h in range(C_HEADS):
            hs = slice(h * C_HEAD_DIM, (h + 1) * C_HEAD_DIM)
            o = jnp.concatenate(outs[bb, h], axis=0)
            o_ref[bb, :, hs] = (_rms(o, gn_ref[...]) * _silu(z_ref[bb, :, hs])).astype(BF16)


def _mixer_c(qkv_c, gates, z, conv_w, alog_lane, dtb_lane, gnorm):
    bsz, t, _ = qkv_c.shape
    tm = TM_GDN
    nb = NB_GDN if bsz % NB_GDN == 0 else 1
    row = lambda b, i: (b, i, 0)
    return pl.pallas_call(
        _gdn_kernel,
        grid=(bsz // nb, t // tm),
        in_specs=[
            pl.BlockSpec((nb, tm, C3_W), row),
            pl.BlockSpec((nb, tm, GATE_W), row),
            pl.BlockSpec((nb, tm, C_W), row),
            _const_spec((C_CONV, C3_W)),
            _const_spec((1, GATE_W)),
            _const_spec((1, GATE_W)),
            _const_spec((1, C_HEAD_DIM)),
        ],
        out_specs=pl.BlockSpec((nb, tm, C_W), row),
        out_shape=jax.ShapeDtypeStruct((bsz, t, C_W), BF16),
        scratch_shapes=[
            pltpu.VMEM((nb, 8, C3_W), F32),
            pltpu.VMEM((nb, C_HEADS, C_HEAD_DIM, C_HEAD_DIM), F32),
            pltpu.VMEM((nb * C_HEADS, SB_GDN, C_HEAD_DIM), F32),
        ],
        compiler_params=_params(("parallel", "arbitrary")),
        name="mixer_c_gated_deltanet",
    )(qkv_c, gates, z, conv_w, alog_lane, dtb_lane, gnorm)


def _ffn_kernel(x_ref, oa_ref, ob_ref, oc_ref, mod_ref, wo_ref, g_ref, wup_ref, cw_ref, wd_ref,
                fg_ref, o_ref,
                hb_ref, acc_ref, ug_ref, uu_ref, act_ref, hg_ref, hu_ref, *, final):
    tm = x_ref.shape[1]

    @pl.when(pl.program_id(1) == 0)
    def _():
        hg_ref[...] = jnp.zeros_like(hg_ref)
        hu_ref[...] = jnp.zeros_like(hu_ref)

    gate1 = mod_ref[0, 2:3, :]
    shift2 = mod_ref[0, 3:4, :]
    scale2 = mod_ref[0, 4:5, :]
    gate2 = mod_ref[0, 5:6, :]
    mix = (jnp.dot(oa_ref[0], wo_ref[0:A_W, :], preferred_element_type=F32)
           + jnp.dot(ob_ref[0], wo_ref[A_W:A_W + B_W, :], preferred_element_type=F32)
           + jnp.dot(oc_ref[0], wo_ref[A_W + B_W:, :], preferred_element_type=F32))
    x1 = x_ref[0] + gate1 * mix
    h = _rms(x1, g_ref[...]) * (1.0 + scale2) + shift2
    hb_ref[...] = h.astype(BF16)
    acc_ref[...] = jnp.zeros_like(acc_ref)

    def width(c):
        return min(FF_CH, D_FF - c * FF_CH)

    def gate_cols(c):
        return slice(c * FF_CH, c * FF_CH + width(c))

    def up_cols(c):
        return slice(D_FF + c * FF_CH, D_FF + c * FF_CH + width(c))

    def up_proj(c):
        hb = hb_ref[...]
        w = width(c)
        ug_ref[c % 2, 8:8 + tm, 0:w] = jnp.dot(hb, wup_ref[:, gate_cols(c)], preferred_element_type=F32)
        uu_ref[c % 2, 8:8 + tm, 0:w] = jnp.dot(hb, wup_ref[:, up_cols(c)], preferred_element_type=F32)

    def conv(u_ref, hdr_ref, c, w):
        n = width(c)
        u = u_ref.at[c % 2]
        u[0:8, 0:n] = hdr_ref[c, :, 0:n]
        y = w[0:1, :] * u[6:6 + tm, 0:n] + w[1:2, :] * u[7:7 + tm, 0:n] + w[2:3, :] * u[8:8 + tm, 0:n]
        hdr_ref[c, :, 0:n] = u[tm:tm + 8, 0:n]
        return y

    up_proj(0)
    for c in range(N_FF_CH + 1):
        if c + 1 < N_FF_CH:
            up_proj(c + 1)
        if c >= 1:
            acc_ref[...] += jnp.dot(act_ref[(c - 1) % 2, :, 0:width(c - 1)], wd_ref[gate_cols(c - 1), :],
                                    preferred_element_type=F32)
        if c < N_FF_CH:
            gate = conv(ug_ref, hg_ref, c, cw_ref[:, gate_cols(c)])
            up = conv(uu_ref, hu_ref, c, cw_ref[:, up_cols(c)])
            act_ref[c % 2, :, 0:width(c)] = (_silu(gate) * up).astype(BF16)
    x2 = x1 + gate2 * acc_ref[...]
    if final:
        x2 = _rms(x2, fg_ref[...])
    o_ref[0] = x2


def _out_ffn(x, o_a, o_b, o_c, mod_l, w_out, g, w_up, conv_w, w_down, final_g, final):
    bsz, t, _ = x.shape
    tm = TM_FFN
    row = lambda b, i: (b, i, 0)
    return pl.pallas_call(
        functools.partial(_ffn_kernel, final=final),
        grid=(bsz, t // tm),
        in_specs=[
            pl.BlockSpec((1, tm, D_MODEL), row),
            pl.BlockSpec((1, tm, A_W), row),
            pl.BlockSpec((1, tm, B_W), row),
            pl.BlockSpec((1, tm, C_W), row),
            pl.BlockSpec((1, 6, D_MODEL), lambda b, i: (b, 0, 0)),
            _const_spec((D_MODEL, D_MODEL)),
            _const_spec((1, D_MODEL)),
            _const_spec((D_MODEL, 2 * D_FF)),
            _const_spec((FFN_CONV, 2 * D_FF)),
            _const_spec((D_FF, D_MODEL)),
            _const_spec((1, D_MODEL)),
        ],
        out_specs=pl.BlockSpec((1, tm, D_MODEL), row),
        out_shape=jax.ShapeDtypeStruct((bsz, t, D_MODEL), F32),
        scratch_shapes=[
            pltpu.VMEM((tm, D_MODEL), BF16),
            pltpu.VMEM((tm, D_MODEL), F32),
            pltpu.VMEM((2, tm + 8, FF_CH), F32),
            pltpu.VMEM((2, tm + 8, FF_CH), F32),
            pltpu.VMEM((2, tm, FF_CH), BF16),
            pltpu.VMEM((N_FF_CH, 8, FF_CH), F32),
            pltpu.VMEM((N_FF_CH, 8, FF_CH), F32),
        ],
        compiler_params=_params(("parallel", "arbitrary")),
        name="outproj_convmlp",
    )(x, o_a, o_b, o_c, mod_l, w_out, g, w_up, conv_w, w_down, final_g)


def _lane_vec(values, start):
    return jnp.zeros((1, GATE_W), F32).at[0, start:start + values.shape[0]].set(values)


def kernel(x, c, ada_w, ada_b, norm_mix_g, norm_ffn_g, w_in, w_out, rel_bias, fgate_bias,
           gdn_conv_w, gdn_A_log, gdn_dt_bias, gdn_norm_g, ffn_w_up, ffn_conv_w,
           ffn_w_down, final_norm_g):
    mod = _modulation(c, ada_w, ada_b)
    final_g = final_norm_g.reshape(1, D_MODEL)
    bias_tiles = _rel_bias_tiles(rel_bias)
    for l in range(DEPTH):
        qkv_ab, qkv_c, z, gates = _inproj(
            x, mod[l], norm_mix_g[l].reshape(1, D_MODEL), _arrange_w_in(w_in[l]))
        f_cols = _fgate(gates, _lane_vec(fgate_bias[l], F_LANE))
        o_a = _mixer_a(qkv_ab, bias_tiles[l])
        o_b = _mixer_b(qkv_ab, f_cols)
        o_c = _mixer_c(qkv_c, gates, z, gdn_conv_w[l],
                       _lane_vec(gdn_A_log[l], A_LANE), _lane_vec(gdn_dt_bias[l], A_LANE),
                       gdn_norm_g[l].reshape(1, C_HEAD_DIM))
        x = _out_ffn(
            x, o_a, o_b, o_c, mod[l], w_out[l].astype(BF16),
            norm_ffn_g[l].reshape(1, D_MODEL),
            ffn_w_up[l].astype(BF16), ffn_conv_w[l], ffn_w_down[l].astype(BF16),
            final_g, final=(l == DEPTH - 1))
    return x
```

```python
import functools

import jax
import jax.numpy as jnp
from jax import lax
from jax.experimental import pallas as pl
from jax.experimental.pallas import tpu as pltpu

F32 = jnp.float32
BF16 = jnp.bfloat16

D_MODEL = 1024
DEPTH = 2
CHUNK = 64
EPS = 1e-6
A_HEADS = 4
A_HEAD_DIM = 64
A_LEFT_CHUNKS = 8
REL_PAST = 256
REL_FUTURE = CHUNK - 1
B_HEADS = 4
B_HEAD_DIM = 64
C_HEADS = 4
C_HEAD_DIM = 128
C_CONV = 4
D_FF = 2816
FFN_CONV = 3
A_W = A_HEADS * A_HEAD_DIM
B_W = B_HEADS * B_HEAD_DIM
C_W = C_HEADS * C_HEAD_DIM
AB_W = 3 * A_W + 3 * B_W
C3_W = 3 * C_W
LANES = 128
GATE_W = LANES
F_LANE, BETA_LANE, A_LANE = 0, 4, 8
NEG = -1e30
LOG2E = 1.4426950408889634
F_PARTS = 3
VMEM_LIMIT = 56 * 1024 * 1024

TM_PROJ = 512
TF_GATE = 1024
TQ_A = 512
TQ_B = 512
TM_GDN = 256
SB_GDN = 128
NB_GDN = 4
TM_FFN = 512
FF_CH = 1280
N_FF_CH = -(-D_FF // FF_CH)


def _sigmoid(x):
    return 1.0 / (1.0 + jnp.exp(-x))


def _silu(x):
    return x * _sigmoid(x)


def _softplus(x):
    return jnp.maximum(x, 0.0) + jnp.log1p(jnp.exp(-jnp.abs(x)))


def _log_sigmoid(x):
    return jnp.minimum(x, 0.0) - jnp.log1p(jnp.exp(-jnp.abs(x)))


def _rms(x, g):
    return x * lax.rsqrt(jnp.mean(x * x, axis=-1, keepdims=True) + EPS) * g


def _params(sem):
    return pltpu.CompilerParams(dimension_semantics=sem, vmem_limit_bytes=VMEM_LIMIT)


def _const_spec(shape):
    nd = len(shape)
    return pl.BlockSpec(shape, lambda *_: (0,) * nd, pipeline_mode=pl.Buffered(1))


def _mod_kernel(c_ref, w_ref, b_ref, o_ref):
    c_act = _silu(c_ref[...])
    o_ref[0] = jnp.dot(c_act, w_ref[0], preferred_element_type=F32) + b_ref[0]


def _modulation(c, ada_w, ada_b):
    bsz = c.shape[0]
    rows = 8
    cp = jnp.zeros((rows, D_MODEL), F32).at[:bsz].set(c)
    ncol = 6 * D_MODEL
    cb = 1536
    out = pl.pallas_call(
        _mod_kernel,
        grid=(DEPTH, ncol // cb),
        in_specs=[
            pl.BlockSpec((rows, D_MODEL), lambda l, j: (0, 0)),
            pl.BlockSpec((1, D_MODEL, cb), lambda l, j: (l, 0, j)),
            pl.BlockSpec((1, 1, cb), lambda l, j: (l, 0, j)),
        ],
        out_specs=pl.BlockSpec((1, rows, cb), lambda l, j: (l, 0, j)),
        out_shape=jax.ShapeDtypeStruct((DEPTH, rows, ncol), F32),
        compiler_params=_params(("arbitrary", "arbitrary")),
        name="adaln_modulation",
    )(cp, ada_w, ada_b.reshape(DEPTH, 1, ncol))
    return out[:, :bsz].reshape(DEPTH, bsz, 6, D_MODEL)


_PROJ_W = AB_W + C3_W + C_W + GATE_W


def _inproj_kernel(x_ref, mod_ref, g_ref, w_ref, oab_ref, oc_ref, oz_ref, og_ref):
    x = x_ref[0]
    shift = mod_ref[0, 0:1, :]
    scale = mod_ref[0, 1:2, :]
    h = _rms(x, g_ref[...]) * (1.0 + scale) + shift
    hb = h.astype(BF16)
    cw = 512
    for j in range(AB_W // cw):
        oab_ref[0, :, j * cw:(j + 1) * cw] = jnp.dot(
            hb, w_ref[:, j * cw:(j + 1) * cw], preferred_element_type=F32).astype(BF16)
    for j in range(C3_W // cw):
        oc_ref[0, :, j * cw:(j + 1) * cw] = jnp.dot(
            hb, w_ref[:, AB_W + j * cw:AB_W + (j + 1) * cw], preferred_element_type=F32)
    oz_ref[0] = jnp.dot(hb, w_ref[:, AB_W + C3_W:AB_W + C3_W + C_W],
                        preferred_element_type=F32)
    og_ref[0] = jnp.dot(hb, w_ref[:, AB_W + C3_W + C_W:], preferred_element_type=F32)


def _inproj(x, mod_l, g, w):
    bsz, t, _ = x.shape
    tm = TM_PROJ
    row = lambda b, i: (b, i, 0)
    return pl.pallas_call(
        _inproj_kernel,
        grid=(bsz, t // tm),
        in_specs=[
            pl.BlockSpec((1, tm, D_MODEL), row),
            pl.BlockSpec((1, 6, D_MODEL), lambda b, i: (b, 0, 0)),
            _const_spec((1, D_MODEL)),
            _const_spec((D_MODEL, _PROJ_W)),
        ],
        out_specs=[
            pl.BlockSpec((1, tm, AB_W), row),
            pl.BlockSpec((1, tm, C3_W), row),
            pl.BlockSpec((1, tm, C_W), row),
            pl.BlockSpec((1, tm, GATE_W), row),
        ],
        out_shape=[
            jax.ShapeDtypeStruct((bsz, t, AB_W), BF16),
            jax.ShapeDtypeStruct((bsz, t, C3_W), F32),
            jax.ShapeDtypeStruct((bsz, t, C_W), F32),
            jax.ShapeDtypeStruct((bsz, t, GATE_W), F32),
        ],
        compiler_params=_params(("parallel", "parallel")),
        name="inproj",
    )(x, mod_l, g, w)


def _arrange_w_in(w_in_l):
    a_end = 3 * A_W
    b_end = a_end + 3 * B_W
    f_end = b_end + B_HEADS
    c_end = f_end + C3_W
    beta_end = c_end + C_HEADS
    a_gate_end = beta_end + C_HEADS
    qa_scale = A_HEAD_DIM ** -0.5
    qb_scale = B_HEAD_DIM ** -0.5 * LOG2E
    w_ab = jnp.concatenate([
        w_in_l[:, 0:A_W] * qa_scale, w_in_l[:, A_W:a_end],
        w_in_l[:, a_end:a_end + B_W] * qb_scale, w_in_l[:, a_end + B_W:b_end]], axis=1)
    w_gate = jnp.concatenate([
        w_in_l[:, b_end:f_end], w_in_l[:, c_end:beta_end], w_in_l[:, beta_end:a_gate_end],
        jnp.zeros((D_MODEL, GATE_W - B_HEADS - 2 * C_HEADS), F32)], axis=1)
    w = jnp.concatenate([w_ab, w_in_l[:, f_end:c_end], w_in_l[:, a_gate_end:], w_gate], axis=1)
    return w.astype(BF16)


def _fgate_kernel(g_ref, fb_ref, o_ref, carry_ref):
    @pl.when(pl.program_id(1) == 0)
    def _():
        carry_ref[...] = jnp.zeros_like(carry_ref)

    tf = g_ref.shape[1]
    lane = lax.broadcasted_iota(jnp.int32, (tf, GATE_W), 1)
    row = lax.broadcasted_iota(jnp.int32, (tf, GATE_W), 0)
    cs = jnp.where(lane < B_HEADS, _log_sigmoid(g_ref[0] + fb_ref[...]), 0.0)
    d = 1
    while d < tf:
        cs = cs + jnp.where(row >= d, pltpu.roll(cs, d, axis=0), 0.0)
        d *= 2
    cs = cs + carry_ref[0:1, :]
    carry_ref[...] = jnp.broadcast_to(cs[tf - 1:tf, :], carry_ref.shape)
    rest = cs * (-LOG2E)
    out = jnp.zeros_like(rest)
    for i in range(F_PARTS):
        part = rest.astype(BF16).astype(F32)
        rest = rest - part
        out = out + (part if i == 0 else pltpu.roll(part, i * B_HEADS, axis=1))
    o_ref[0] = out.astype(BF16)


def _fgate(gates, fbias_lane):
    bsz, t, _ = gates.shape
    tf = min(TF_GATE, t)
    return pl.pallas_call(
        _fgate_kernel,
        grid=(bsz, t // tf),
        in_specs=[
            pl.BlockSpec((1, tf, GATE_W), lambda b, i: (b, i, 0)),
            _const_spec((1, GATE_W)),
        ],
        out_specs=pl.BlockSpec((1, tf, GATE_W), lambda b, i: (b, i, 0)),
        out_shape=jax.ShapeDtypeStruct((bsz, t, GATE_W), BF16),
        scratch_shapes=[pltpu.VMEM((8, LANES), F32)],
        compiler_params=_params(("parallel", "arbitrary")),
        name="fgate_cumsum",
    )(gates, fbias_lane)


def _mixa_kernel(q_ref, kp_ref, kc_ref, vp_ref, vc_ref, bias_ref, o_ref):
    i = pl.program_id(1)
    tq = q_ref.shape[1]
    lane = lax.broadcasted_iota(jnp.int32, (1, LANES), 1)
    lo = lane < A_HEAD_DIM
    has_prev = i > 0
    nt = (((1,), (1,)), ((), ()))
    npair = A_HEADS // 2
    logits, values = [], []
    for pair in range(npair):
        sl = slice(pair * LANES, (pair + 1) * LANES)
        q = q_ref[0, :, sl]
        zero = jnp.zeros_like(q)
        q_both = jnp.concatenate([jnp.where(lo, q, zero), jnp.where(lo, zero, q)], axis=0)
        k_both = jnp.concatenate([kp_ref[0, :, sl], kc_ref[0, :, sl]], axis=0)
        values.append(jnp.concatenate([vp_ref[0, :, sl], vc_ref[0, :, sl]], axis=0))
        logits.append(lax.dot_general(q_both, k_both, nt, preferred_element_type=F32))
    for pair in range(npair):
        probs, norms = [], []
        for hh in range(2):
            h = 2 * pair + hh
            s = logits[pair][hh * tq:(hh + 1) * tq]
            s_p = jnp.where(has_prev, s[:, 0:tq] + bias_ref[h, :, 0:tq], NEG)
            s_c = s[:, tq:2 * tq] + bias_ref[h, :, tq:2 * tq]
            m = jnp.maximum(jnp.max(s_p, axis=-1, keepdims=True), jnp.max(s_c, axis=-1, keepdims=True))
            p_p = jnp.exp(s_p - m)
            p_c = jnp.exp(s_c - m)
            norms.append(jnp.sum(p_p, axis=-1, keepdims=True) + jnp.sum(p_c, axis=-1, keepdims=True))
            probs.append(jnp.concatenate([p_p, p_c], axis=1).astype(BF16))
        o = jnp.dot(jnp.concatenate(probs, axis=0), values[pair], preferred_element_type=F32)
        o_ref[0, :, pair * LANES:(pair + 1) * LANES] = jnp.where(
            lo, o[0:tq] / norms[0], o[tq:2 * tq] / norms[1]).astype(BF16)


def _mixer_a(qkv_ab, bias):
    bsz, t, _ = qkv_ab.shape
    tq = TQ_A
    prev = lambda col: (lambda b, i: (b, jnp.maximum(i - 1, 0), col))
    cur = lambda col: (lambda b, i: (b, i, col))
    blk = (1, tq, A_W)
    return pl.pallas_call(
        _mixa_kernel,
        grid=(bsz, t // tq),
        in_specs=[
            pl.BlockSpec(blk, cur(0)),
            pl.BlockSpec(blk, prev(1)), pl.BlockSpec(blk, cur(1)),
            pl.BlockSpec(blk, prev(2)), pl.BlockSpec(blk, cur(2)),
            _const_spec((A_HEADS, tq, 2 * tq)),
        ],
        out_specs=pl.BlockSpec(blk, cur(0)),
        out_shape=jax.ShapeDtypeStruct((bsz, t, A_W), BF16),
        compiler_params=_params(("parallel", "parallel")),
        name="mixer_a_chunk_attn",
    )(qkv_ab, qkv_ab, qkv_ab, qkv_ab, qkv_ab, bias)


_REL_PAD = 384
_REL_ROLL_W = 4 * TQ_A


def _relbias_kernel(tab_ref, o_ref, row_ref):
    tq = o_ref.shape[1]
    m = lax.broadcasted_iota(jnp.int32, (_REL_PAD, _REL_ROLL_W), 1)
    m = jnp.where(m < _REL_ROLL_W // 2, m, m - _REL_ROLL_W)
    idx = jnp.clip(tq - m, -REL_FUTURE, REL_PAST) + REL_FUTURE
    n = lax.broadcasted_iota(jnp.int32, (_REL_PAD, _REL_ROLL_W), 0)
    onehot = (n == idx).astype(F32)
    row_ref[...] = jnp.dot(tab_ref[...], onehot, precision=lax.Precision.HIGHEST,
                           preferred_element_type=F32)
    row = row_ref[pl.ds(pl.program_id(0), 1), :]
    rolled = pltpu.roll(jnp.broadcast_to(row, (tq, _REL_ROLL_W)), 0, 1, stride=1, stride_axis=0)
    qc = lax.broadcasted_iota(jnp.int32, (tq, 2 * tq), 0) // CHUNK
    kc = lax.broadcasted_iota(jnp.int32, (tq, 2 * tq), 1) // CHUNK - tq // CHUNK
    valid = (kc <= qc) & (kc >= qc - A_LEFT_CHUNKS)
    o_ref[0] = jnp.where(valid, rolled[:, 0:2 * tq], NEG)


def _rel_bias_tiles(rel_bias):
    nrel = rel_bias.shape[-1]
    nrow = DEPTH * A_HEADS
    tab = jnp.zeros((nrow, _REL_PAD), F32).at[:, :nrel].set(rel_bias.reshape(nrow, nrel))
    out = pl.pallas_call(
        _relbias_kernel,
        grid=(nrow,),
        in_specs=[pl.BlockSpec((nrow, _REL_PAD), lambda h: (0, 0))],
        out_specs=pl.BlockSpec((1, TQ_A, 2 * TQ_A), lambda h: (h, 0, 0)),
        out_shape=jax.ShapeDtypeStruct((nrow, TQ_A, 2 * TQ_A), F32),
        scratch_shapes=[pltpu.VMEM((nrow, _REL_ROLL_W), F32)],
        compiler_params=_params(("arbitrary",)),
        name="relbias_toeplitz",
    )(tab)
    return out.reshape(DEPTH, A_HEADS, TQ_A, 2 * TQ_A)


def _mixb_kernel(q_ref, k_ref, v_ref, f_ref, o_ref, s_ref, p_ref, m_ref, l_ref, acc_ref):
    pair = pl.program_id(1)
    tq = TQ_B
    tk = tq
    lax.fori_loop(0, q_ref.shape[1] // tq, functools.partial(
        _mixb_query_tile, pair, q_ref, k_ref, v_ref, f_ref, o_ref, s_ref, p_ref, m_ref, l_ref, acc_ref), 0)


def _mixb_query_tile(pair, q_ref, k_ref, v_ref, f_ref, o_ref, s_ref, p_ref, m_ref, l_ref, acc_ref,
                     i, carry):
    tq = TQ_B
    tk = tq
    row0 = pl.multiple_of(i * tq, tq)
    lane = lax.broadcasted_iota(jnp.int32, (1, LANES), 1)
    lo = lane < B_HEAD_DIM
    q = q_ref[0, pl.ds(row0, tq), :]
    zero = jnp.zeros_like(q)
    qs = []
    for hh in range(2):
        head = 2 * pair + hh
        picks = functools.reduce(jnp.logical_or, [lane == head + t * B_HEADS for t in range(F_PARTS)])
        ones = jnp.broadcast_to(jnp.where(picks, 1.0, 0.0).astype(F32), (tq, LANES)).astype(BF16)
        qs.append(jnp.concatenate([jnp.where(lo if hh == 0 else ~lo, q, zero), ones], axis=1))
    nt = (((1,), (1,)), ((), ()))
    m_ref[...] = jnp.full(m_ref.shape, NEG, F32)
    l_ref[...] = jnp.zeros(l_ref.shape, F32)
    acc_ref[...] = jnp.zeros(acc_ref.shape, F32)
    p_ref[...] = jnp.zeros(p_ref.shape, BF16)

    def logits(j):
        off = pl.multiple_of(j * tk, tk)
        kf = jnp.concatenate([k_ref[0, pl.ds(off, tk), :], f_ref[0, pl.ds(off, tk), :]], axis=1)
        return [lax.dot_general(qs[hh], kf, nt, preferred_element_type=F32) for hh in range(2)]

    def pending_pv(j):
        off = pl.multiple_of(jnp.maximum(j - 1, 0) * tk, tk)
        v = v_ref[0, pl.ds(off, tk), :]
        return [jnp.dot(p_ref[hh], v, preferred_element_type=F32) for hh in range(2)]

    def softmax(hh, diagonal):
        s = s_ref[hh]
        if diagonal:
            r = lax.broadcasted_iota(jnp.int32, (tq, tk), 0)
            c = lax.broadcasted_iota(jnp.int32, (tq, tk), 1)
            s = jnp.where(c <= r, s, NEG)
        tiles = [s[:, t * LANES:(t + 1) * LANES] for t in range(tk // LANES)]
        m_tile = functools.reduce(jnp.maximum, tiles)
        m_prev = m_ref[hh]
        m_new = jnp.maximum(m_prev, jnp.max(m_tile, axis=-1, keepdims=True))
        alpha = jnp.exp2(m_prev - m_new)
        p_tiles = [jnp.exp2(t - m_new) for t in tiles]
        l_ref[hh] = alpha * l_ref[hh] + functools.reduce(jnp.add, p_tiles)
        m_ref[hh] = m_new
        return alpha, jnp.concatenate(p_tiles, axis=1).astype(BF16)

    def store(s_pair):
        for hh in range(2):
            s_ref[hh] = s_pair[hh]

    store(logits(0))

    def body(j, carry):
        pv = pending_pv(j)
        nxt = logits(j + 1)
        for hh in range(2):
            alpha, p = softmax(hh, False)
            p_ref[hh] = p
            acc_ref[hh] = alpha * (acc_ref[hh] + pv[hh])
        store(nxt)
        return carry

    lax.fori_loop(0, i, body, 0)
    pv = pending_pv(i)
    v = v_ref[0, pl.ds(pl.multiple_of(i * tk, tk), tk), :]
    outs = []
    for hh in range(2):
        alpha, p = softmax(hh, True)
        acc = alpha * (acc_ref[hh] + pv[hh]) + jnp.dot(p, v, preferred_element_type=F32)
        outs.append(acc / jnp.sum(l_ref[hh], axis=-1, keepdims=True))
    o_ref[0, pl.ds(row0, tq), :] = jnp.where(lo, outs[0], outs[1]).astype(BF16)
    return carry


def _mixer_b(qkv_ab, f_cols):
    bsz, t, _ = qkv_ab.shape
    tq = TQ_B
    base = 3 * A_W // LANES
    npair = B_HEADS // 2
    whole = lambda col: (lambda b, p: (b, 0, col + p))
    return pl.pallas_call(
        _mixb_kernel,
        grid=(bsz, npair),
        in_specs=[
            pl.BlockSpec((1, t, LANES), whole(base)),
            pl.BlockSpec((1, t, LANES), whole(base + npair)),
            pl.BlockSpec((1, t, LANES), whole(base + 2 * npair)),
            pl.BlockSpec((1, t, GATE_W), lambda b, p: (b, 0, 0)),
        ],
        out_specs=pl.BlockSpec((1, t, LANES), whole(0)),
        out_shape=jax.ShapeDtypeStruct((bsz, t, B_W), BF16),
        scratch_shapes=[
            pltpu.VMEM((2, tq, tq), F32),
            pltpu.VMEM((2, tq, tq), BF16),
            pltpu.VMEM((2, tq, LANES), F32),
            pltpu.VMEM((2, tq, LANES), F32),
            pltpu.VMEM((2, tq, LANES), F32),
        ],
        compiler_params=_params(("parallel", "parallel")),
        name="mixer_b_forgetting_attn",
    )(qkv_ab, qkv_ab, qkv_ab, f_cols)


def _gdn_kernel(x_ref, gate_ref, z_ref, cw_ref, alog_ref, dtb_ref, gn_ref, o_ref,
                xs_ref, s_ref, vn_ref):
    nb, tm = x_ref.shape[0], x_ref.shape[1]
    sb = SB_GDN
    nsub = tm // sb
    cps = sb // CHUNK

    @pl.when(pl.program_id(1) == 0)
    def _():
        xs_ref[...] = jnp.zeros_like(xs_ref)
        s_ref[...] = jnp.zeros_like(s_ref)

    ri = lax.broadcasted_iota(jnp.int32, (sb, sb), 0)
    ci = lax.broadcasted_iota(jnp.int32, (sb, sb), 1)
    same = (ri // CHUNK) == (ci // CHUNK)
    tril = same & (ci <= ri)
    strict = same & (ci < ri)
    eye = (ri == ci).astype(F32)

    def level_mask(b):
        return ((ri // (2 * b)) == (ci // (2 * b))) & (((ri // b) % 2) == 1) & (((ci // b) % 2) == 0)

    pos = lax.broadcasted_iota(jnp.int32, (tm, GATE_W), 0) % CHUNK
    row8 = lax.broadcasted_iota(jnp.int32, (8, C3_W), 0)
    sub_chunk = lax.broadcasted_iota(jnp.int32, (sb, 1), 0) // CHUNK
    nt = (((1,), (1,)), ((), ()))
    tn = (((0,), (0,)), ((), ()))

    units = [(bb, sub, h) for bb in range(nb) for sub in range(nsub) for h in range(C_HEADS)]
    gates = {}
    qkvs = {}
    for bb in range(nb):
        xt = x_ref[bb]
        tail = xs_ref[bb]
        y = cw_ref[C_CONV - 1:C_CONV, :] * xt
        for k in range(1, C_CONV):
            head = jnp.where(row8 < k, pltpu.roll(tail, k, axis=0), pltpu.roll(xt[0:8], k, axis=0))
            shifted = jnp.concatenate([head, pltpu.roll(xt, k, axis=0)[8:]], axis=0)
            y = y + cw_ref[C_CONV - 1 - k:C_CONV - k, :] * shifted
        xs_ref[bb] = xt[tm - 8:tm]
        qkvs[bb] = _silu(y)

        gt = gate_ref[bb]
        beta = _sigmoid(gt)
        g = -jnp.exp(alog_ref[...]) * _softplus(gt + dtb_ref[...])
        gc = g
        d = 1
        while d < CHUNK:
            gc = gc + jnp.where(pos >= d, pltpu.roll(gc, d, axis=0), 0.0)
            d *= 2
        gl = jnp.concatenate(
            [jnp.broadcast_to(gc[(c + 1) * CHUNK - 1:(c + 1) * CHUNK, :], (CHUNK, GATE_W))
             for c in range(tm // CHUNK)], axis=0)
        gates[bb] = (beta, gc, gl, gc.T)

    work = {}
    for u in units:
        bb, sub, h = u
        rows = slice(sub * sb, (sub + 1) * sb)
        qkv = qkvs[bb]
        beta, gc, gl, gc_t = gates[bb]
        qh = qkv[rows, h * C_HEAD_DIM:(h + 1) * C_HEAD_DIM]
        kh = qkv[rows, C_W + h * C_HEAD_DIM:C_W + (h + 1) * C_HEAD_DIM]
        vh = qkv[rows, 2 * C_W + h * C_HEAD_DIM:2 * C_W + (h + 1) * C_HEAD_DIM]
        qh = qh * lax.rsqrt(jnp.sum(qh * qh, axis=-1, keepdims=True) + EPS) * (C_HEAD_DIM ** -0.5)
        kh = kh * lax.rsqrt(jnp.sum(kh * kh, axis=-1, keepdims=True) + EPS)
        b_col = beta[rows, BETA_LANE + h:BETA_LANE + h + 1]
        gc_col = gc[rows, A_LANE + h:A_LANE + h + 1]
        gl_col = gl[rows, A_LANE + h:A_LANE + h + 1]
        gc_row = gc_t[A_LANE + h:A_LANE + h + 1, rows]
        kbeta = kh * b_col
        lm = jnp.where(tril, jnp.exp(gc_col - gc_row), 0.0)
        kq = lax.dot_general(jnp.concatenate([kbeta, qh], axis=0).astype(BF16), kh.astype(BF16),
                             nt, preferred_element_type=F32)
        amat = jnp.where(strict, kq[0:sb] * lm, 0.0)
        work[u] = dict(
            amat16=amat.astype(BF16), attn=jnp.where(tril, kq[sb:2 * sb] * lm, 0.0),
            rhs=jnp.concatenate([vh * b_col, kbeta * jnp.exp(gc_col)], axis=1).astype(BF16),
            qg=qh * jnp.exp(gc_col), kd=kh * jnp.exp(gl_col - gc_col), decay=jnp.exp(gl_col),
            tinv=eye - jnp.where(level_mask(1), amat, 0.0))

    zero16 = jnp.zeros((sb, sb), BF16)
    b = 2
    while b < CHUNK:
        mask = level_mask(b)
        tas = {}
        for u in units:
            t16 = work[u]["tinv"].astype(BF16)
            work[u]["t16"] = t16
            tas[u] = jnp.dot(t16, jnp.where(mask, work[u]["amat16"], zero16),
                             preferred_element_type=F32)
        for u in units:
            work[u]["tinv"] = work[u]["tinv"] - jnp.dot(tas[u].astype(BF16), work[u]["t16"],
                                                        preferred_element_type=F32)
        b *= 2
    for u in units:
        uw = jnp.dot(work[u]["tinv"].astype(BF16), work[u]["rhs"], preferred_element_type=F32)
        work[u]["u"] = uw[:, :C_HEAD_DIM]
        work[u]["w"] = uw[:, C_HEAD_DIM:]

    chains = [(bb, h) for bb in range(nb) for h in range(C_HEADS)]
    outs = {ch: [] for ch in chains}
    for sub in range(nsub):
        for bb, h in chains:
            vn_ref[bb * C_HEADS + h] = work[bb, sub, h]["u"]
        for cc in range(cps):
            rs = slice(cc * CHUNK, (cc + 1) * CHUNK)
            in_chunk = sub_chunk == cc
            first = {}
            for bb, h in chains:
                ph = work[bb, sub, h]
                s_mat = s_ref[bb, h]
                first[bb, h] = (s_mat, jnp.dot(
                    jnp.concatenate([ph["w"][rs], ph["qg"][rs]], axis=0).astype(BF16),
                    s_mat.astype(BF16), preferred_element_type=F32))
            for bb, h in chains:
                ph = work[bb, sub, h]
                s_mat, ws = first[bb, h]
                vn = vn_ref.at[bb * C_HEADS + h]
                vn[rs, :] = ph["u"][rs] - ws[0:CHUNK]
                vn_all = vn[...].astype(BF16)
                outs[bb, h].append(ws[CHUNK:] + jnp.dot(ph["attn"][rs].astype(BF16), vn_all,
                                                        preferred_element_type=F32))
                kd = jnp.where(in_chunk, ph["kd"], 0.0).astype(BF16)
                s_ref[bb, h] = s_mat * ph["decay"][cc * CHUNK:cc * CHUNK + 1] + lax.dot_general(
                    kd, vn_all, tn, preferred_element_type=F32)

    for bb in range(nb):
        for h in range(C_HEADS):
            hs = slice(h * C_HEAD_DIM, (h + 1) * C_HEAD_DIM)
            o = jnp.concatenate(outs[bb, h], axis=0)
            o_ref[bb, :, hs] = (_rms(o, gn_ref[...]) * _silu(z_ref[bb, :, hs])).astype(BF16)


def _mixer_c(qkv_c, gates, z, conv_w, alog_lane, dtb_lane, gnorm):
    bsz, t, _ = qkv_c.shape
    tm = TM_GDN
    nb = NB_GDN if bsz % NB_GDN == 0 else 1
    row = lambda b, i: (b, i, 0)
    return pl.pallas_call(
        _gdn_kernel,
        grid=(bsz // nb, t // tm),
        in_specs=[
            pl.BlockSpec((nb, tm, C3_W), row),
            pl.BlockSpec((nb, tm, GATE_W), row),
            pl.BlockSpec((nb, tm, C_W), row),
            _const_spec((C_CONV, C3_W)),
            _const_spec((1, GATE_W)),
            _const_spec((1, GATE_W)),
            _const_spec((1, C_HEAD_DIM)),
        ],
        out_specs=pl.BlockSpec((nb, tm, C_W), row),
        out_shape=jax.ShapeDtypeStruct((bsz, t, C_W), BF16),
        scratch_shapes=[
            pltpu.VMEM((nb, 8, C3_W), F32),
            pltpu.VMEM((nb, C_HEADS, C_HEAD_DIM, C_HEAD_DIM), F32),
            pltpu.VMEM((nb * C_HEADS, SB_GDN, C_HEAD_DIM), F32),
        ],
        compiler_params=_params(("parallel", "arbitrary")),
        name="mixer_c_gated_deltanet",
    )(qkv_c, gates, z, conv_w, alog_lane, dtb_lane, gnorm)


def _ffn_kernel(x_ref, oa_ref, ob_ref, oc_ref, mod_ref, wo_ref, g_ref, wup_ref, cw_ref, wd_ref,
                fg_ref, o_ref,
                hb_ref, acc_ref, ug_ref, uu_ref, act_ref, hg_ref, hu_ref, *, final):
    tm = x_ref.shape[1]

    @pl.when(pl.program_id(1) == 0)
    def _():
        hg_ref[...] = jnp.zeros_like(hg_ref)
        hu_ref[...] = jnp.zeros_like(hu_ref)

    gate1 = mod_ref[0, 2:3, :]
    shift2 = mod_ref[0, 3:4, :]
    scale2 = mod_ref[0, 4:5, :]
    gate2 = mod_ref[0, 5:6, :]
    mix = (jnp.dot(oa_ref[0], wo_ref[0:A_W, :], preferred_element_type=F32)
           + jnp.dot(ob_ref[0], wo_ref[A_W:A_W + B_W, :], preferred_element_type=F32)
           + jnp.dot(oc_ref[0], wo_ref[A_W + B_W:, :], preferred_element_type=F32))
    x1 = x_ref[0] + gate1 * mix
    h = _rms(x1, g_ref[...]) * (1.0 + scale2) + shift2
    hb_ref[...] = h.astype(BF16)
    acc_ref[...] = jnp.zeros_like(acc_ref)

    def width(c):
        return min(FF_CH, D_FF - c * FF_CH)

    def gate_cols(c):
        return slice(c * FF_CH, c * FF_CH + width(c))

    def up_cols(c):
        return slice(D_FF + c * FF_CH, D_FF + c * FF_CH + width(c))

    def up_proj(c):
        hb = hb_ref[...]
        w = width(c)
        ug_ref[c % 2, 8:8 + tm, 0:w] = jnp.dot(hb, wup_ref[:, gate_cols(c)], preferred_element_type=F32)
        uu_ref[c % 2, 8:8 + tm, 0:w] = jnp.dot(hb, wup_ref[:, up_cols(c)], preferred_element_type=F32)

    def conv(u_ref, hdr_ref, c, w):
        n = width(c)
        u = u_ref.at[c % 2]
        u[0:8, 0:n] = hdr_ref[c, :, 0:n]
        y = w[0:1, :] * u[6:6 + tm, 0:n] + w[1:2, :] * u[7:7 + tm, 0:n] + w[2:3, :] * u[8:8 + tm, 0:n]
        hdr_ref[c, :, 0:n] = u[tm:tm + 8, 0:n]
        return y

    up_proj(0)
    for c in range(N_FF_CH + 1):
        if c + 1 < N_FF_CH:
            up_proj(c + 1)
        if c >= 1:
            acc_ref[...] += jnp.dot(act_ref[(c - 1) % 2, :, 0:width(c - 1)], wd_ref[gate_cols(c - 1), :],
                                    preferred_element_type=F32)
        if c < N_FF_CH:
            gate = conv(ug_ref, hg_ref, c, cw_ref[:, gate_cols(c)])
            up = conv(uu_ref, hu_ref, c, cw_ref[:, up_cols(c)])
            act_ref[c % 2, :, 0:width(c)] = (_silu(gate) * up).astype(BF16)
    x2 = x1 + gate2 * acc_ref[...]
    if final:
        x2 = _rms(x2, fg_ref[...])
    o_ref[0] = x2


def _out_ffn(x, o_a, o_b, o_c, mod_l, w_out, g, w_up, conv_w, w_down, final_g, final):
    bsz, t, _ = x.shape
    tm = TM_FFN
    row = lambda b, i: (b, i, 0)
    return pl.pallas_call(
        functools.partial(_ffn_kernel, final=final),
        grid=(bsz, t // tm),
        in_specs=[
            pl.BlockSpec((1, tm, D_MODEL), row),
            pl.BlockSpec((1, tm, A_W), row),
            pl.BlockSpec((1, tm, B_W), row),
            pl.BlockSpec((1, tm, C_W), row),
            pl.BlockSpec((1, 6, D_MODEL), lambda b, i: (b, 0, 0)),
            _const_spec((D_MODEL, D_MODEL)),
            _const_spec((1, D_MODEL)),
            _const_spec((D_MODEL, 2 * D_FF)),
            _const_spec((FFN_CONV, 2 * D_FF)),
            _const_spec((D_FF, D_MODEL)),
            _const_spec((1, D_MODEL)),
        ],
        out_specs=pl.BlockSpec((1, tm, D_MODEL), row),
        out_shape=jax.ShapeDtypeStruct((bsz, t, D_MODEL), F32),
        scratch_shapes=[
            pltpu.VMEM((tm, D_MODEL), BF16),
            pltpu.VMEM((tm, D_MODEL), F32),
            pltpu.VMEM((2, tm + 8, FF_CH), F32),
            pltpu.VMEM((2, tm + 8, FF_CH), F32),
            pltpu.VMEM((2, tm, FF_CH), BF16),
            pltpu.VMEM((N_FF_CH, 8, FF_CH), F32),
            pltpu.VMEM((N_FF_CH, 8, FF_CH), F32),
        ],
        compiler_params=_params(("parallel", "arbitrary")),
        name="outproj_convmlp",
    )(x, o_a, o_b, o_c, mod_l, w_out, g, w_up, conv_w, w_down, final_g)


def _lane_vec(values, start):
    return jnp.zeros((1, GATE_W), F32).at[0, start:start + values.shape[0]].set(values)


def kernel(x, c, ada_w, ada_b, norm_mix_g, norm_ffn_g, w_in, w_out, rel_bias, fgate_bias,
           gdn_conv_w, gdn_A_log, gdn_dt_bias, gdn_norm_g, ffn_w_up, ffn_conv_w,
           ffn_w_down, final_norm_g):
    mod = _modulation(c, ada_w, ada_b)
    final_g = final_norm_g.reshape(1, D_MODEL)
    bias_tiles = _rel_bias_tiles(rel_bias)
    for l in range(DEPTH):
        qkv_ab, qkv_c, z, gates = _inproj(
            x, mod[l], norm_mix_g[l].reshape(1, D_MODEL), _arrange_w_in(w_in[l]))
        f_cols = _fgate(gates, _lane_vec(fgate_bias[l], F_LANE))
        o_a = _mixer_a(qkv_ab, bias_tiles[l])
        o_b = _mixer_b(qkv_ab, f_cols)
        o_c = _mixer_c(qkv_c, gates, z, gdn_conv_w[l],
                       _lane_vec(gdn_A_log[l], A_LANE), _lane_vec(gdn_dt_bias[l], A_LANE),
                       gdn_norm_g[l].reshape(1, C_HEAD_DIM))
        x = _out_ffn(
            x, o_a, o_b, o_c, mod[l], w_out[l].astype(BF16),
            norm_ffn_g[l].reshape(1, D_MODEL),
            ffn_w_up[l].astype(BF16), ffn_conv_w[l], ffn_w_down[l].astype(BF16),
            final_g, final=(l == DEPTH - 1))
    return x
```
